```python
import jax, jax.numpy as jnp
from jax import lax
import numpy as np

D_MODEL = 1024
BATCH = 4
SEQ = 4096
DEPTH = 1

GRID_W = 64
CTX_LEN = 256
FF_HALF = 2816
F_GROUPS = 4
F_GROUP_DIM = 128
F_WIDTH = F_GROUPS * F_GROUP_DIM
M_HEADS = 4
M_HEAD_DIM = 256
M_WIDTH = M_HEADS * M_HEAD_DIM
CONV_K = 3
CHUNK = 128
N_ADA = 9
EPS = 1e-6

COL_F = 0
COL_Q = COL_F + F_WIDTH
COL_K = COL_Q + M_WIDTH
COL_V = COL_K + M_WIDTH
COL_O = COL_V + M_WIDTH
COL_GATES = COL_O + M_WIDTH
COL_BR = COL_GATES + 4 * M_HEADS
IN_WIDTH = COL_BR + 2 * D_MODEL

kernel_name = "hybrid_fourier_mlstm_dit_layer"


def rmsnorm(x, g):
    xf = x.astype(jnp.float32)
    y = xf * lax.rsqrt(jnp.mean(xf * xf, axis=-1, keepdims=True) + EPS)
    return (y * g.astype(jnp.float32)).astype(x.dtype)


def modulate(x, shift, scale):
    return x * (1 + scale) + shift


def ada_params(cvec, w, b):
    m = jax.nn.silu(cvec) @ w + b
    return m.reshape(cvec.shape[0], N_ADA, 1, D_MODEL)


def swiglu(u, w13, w2):
    a, b = jnp.split(u @ w13, 2, axis=-1)
    return (jax.nn.silu(a) * b) @ w2


def half_ffn(h, shift, scale, gate, g_pre, g_post, w13, w2):
    u = modulate(rmsnorm(h, g_pre), shift, scale)
    return h + 0.5 * gate * rmsnorm(swiglu(u, w13, w2), g_post)


def dwconv(x, w, b):
    y = lax.conv_general_dilated(
        x, w[:, None, :].astype(x.dtype), window_strides=(1,),
        padding=[(CONV_K // 2, CONV_K // 2)],
        dimension_numbers=("NWC", "WIO", "NWC"),
        feature_group_count=x.shape[-1])
    return y + b


def to_heads(x):
    B, T, _ = x.shape
    return x.reshape(B, T, M_HEADS, M_HEAD_DIM).transpose(0, 2, 1, 3)


def fourier_latent(xf):
    B, T, _ = xf.shape
    rows = T // GRID_W
    z = xf.astype(jnp.float32).reshape(B, rows, GRID_W, F_GROUPS, F_GROUP_DIM)
    y = jnp.fft.fftn(z, axes=(1, 2, 4), norm="ortho").real
    return y.reshape(B, T, F_WIDTH).astype(xf.dtype)


def fourier_context(xf):
    B, T, _ = xf.shape
    z = xf.astype(jnp.float32).reshape(B, T, F_GROUPS, F_GROUP_DIM)
    y = jnp.fft.fftn(z, axes=(1, 3), norm="ortho").real
    return y.reshape(B, T, F_WIDTH).astype(xf.dtype)


def project(u, w_in, b_in, conv_w, conv_b):
    B, T, _ = u.shape
    p = u @ w_in + b_in
    xf = p[..., COL_F:COL_Q]
    qk = jax.nn.silu(dwconv(p[..., COL_Q:COL_V], conv_w, conv_b))
    q = to_heads(qk[..., :M_WIDTH])
    k = to_heads(qk[..., M_WIDTH:])
    v = to_heads(p[..., COL_V:COL_O])
    o = p[..., COL_O:COL_GATES]
    gates = p[..., COL_GATES:COL_BR].astype(jnp.float32)
    gates = gates.reshape(B, T, 4, M_HEADS).transpose(2, 0, 3, 1)
    gate_pre = (gates[0], jax.nn.log_sigmoid(gates[1]), gates[2], jax.nn.log_sigmoid(gates[3]))
    g_f = p[..., COL_BR:COL_BR + D_MODEL]
    g_m = p[..., COL_BR + D_MODEL:]
    return xf, q, k, v, o, gate_pre, g_f, g_m


def mlstm_scan(q, k, v, li, lf, state):
    B, H, T, DK = q.shape
    NC = T // CHUNK

    def chunks(a):
        a = a.astype(jnp.float32).reshape(B, H, NC, CHUNK, *a.shape[3:])
        return jnp.moveaxis(a, 2, 0)

    qc, kc, vc, lic, lfc = map(chunks, (q, k * (DK ** -0.5), v, li, lf))
    lower = jnp.tril(jnp.ones((CHUNK, CHUNK), dtype=bool))

    def step(carry, inp):
        C, n, m = carry
        qx, kx, vx, lix, lfx = inp
        b = jnp.cumsum(lfx, axis=-1)
        dmat = b[..., :, None] - b[..., None, :] + lix[..., None, :]
        dmat = jnp.where(lower, dmat, -jnp.inf)
        inter = b + m[..., None]
        m_t = jnp.maximum(inter, jnp.max(dmat, axis=-1))
        w = jnp.exp(dmat - m_t[..., None])
        a = jnp.exp(inter - m_t)
        s = jnp.einsum("bhtd,bhsd->bhts", qx, kx) * w
        num = a[..., None] * jnp.einsum("bhtd,bhde->bhte", qx, C) + jnp.einsum("bhts,bhse->bhte", s, vx)
        den = a * jnp.einsum("bhtd,bhd->bht", qx, n) + jnp.sum(s, axis=-1)
        h = num / jnp.maximum(jnp.abs(den), jnp.exp(-m_t))[..., None]
        bL = b[..., -1]
        g = bL[..., None] - b + lix
        m_new = jnp.maximum(bL + m, jnp.max(g, axis=-1))
        decay = jnp.exp(bL + m - m_new)
        wk = jnp.exp(g - m_new[..., None])
        C_new = decay[..., None, None] * C + jnp.einsum("bhsd,bhse->bhde", kx * wk[..., None], vx)
        n_new = decay[..., None] * n + jnp.einsum("bhs,bhsd->bhd", wk, kx)
        return (C_new, n_new, m_new), h

    state_out, hs = lax.scan(step, state, (qc, kc, vc, lic, lfc))
    h = jnp.moveaxis(hs, 0, 2).reshape(B, H, T, -1)
    return h.astype(q.dtype), state_out


def bidir_mlstm(q, k, v, gate_pre, state_f, state_b):
    li_f, lf_f, li_b, lf_b = gate_pre
    h_f, s_f = mlstm_scan(q, k, v, li_f, lf_f, state_f)
    flip = lambda a: jnp.flip(a, axis=2)
    h_b, s_b = mlstm_scan(flip(q), flip(k), flip(v), flip(li_b), flip(lf_b), state_b)
    return h_f + flip(h_b), s_f, s_b


def merge(y_fourier, h, o, g_f, g_m, head_g, w_four, w_mproj, w_out):
    B, H, T, dh = h.shape
    hn = rmsnorm(h.transpose(0, 2, 1, 3), head_g.reshape(M_HEADS, M_HEAD_DIM)).reshape(B, T, M_WIDTH)
    hm = jax.nn.sigmoid(o) * hn
    y = jax.nn.sigmoid(g_f) * (y_fourier @ w_four) + jax.nn.sigmoid(g_m) * (hm @ w_mproj)
    return y @ w_out


def token_mixing(h_lat, h_ctx, mod_l, mod_c, g_pre, g_post, w_in, b_in, conv_w, conv_b,
                 head_g, w_four, w_mproj, w_out, update_ctx):
    sh_l, sc_l, gt_l = mod_l
    sh_c, sc_c, gt_c = mod_c
    pl = project(modulate(rmsnorm(h_lat, g_pre), sh_l, sc_l), w_in, b_in, conv_w, conv_b)
    pc = project(modulate(rmsnorm(h_ctx, g_pre), sh_c, sc_c), w_in, b_in, conv_w, conv_b)
    B = h_ctx.shape[0]
    zero = (jnp.zeros((B, M_HEADS, M_HEAD_DIM, M_HEAD_DIM), jnp.float32),
            jnp.zeros((B, M_HEADS, M_HEAD_DIM), jnp.float32),
            jnp.zeros((B, M_HEADS), jnp.float32))
    h_c, s_f, s_b = bidir_mlstm(pc[1], pc[2], pc[3], pc[5], zero, zero)
    h_l, _, _ = bidir_mlstm(pl[1], pl[2], pl[3], pl[5], s_f, s_b)
    out_l = merge(fourier_latent(pl[0]), h_l, pl[4], pl[6], pl[7], head_g, w_four, w_mproj, w_out)
    new_lat = h_lat + gt_l * rmsnorm(out_l, g_post)
    new_ctx = None
    if update_ctx:
        out_c = merge(fourier_context(pc[0]), h_c, pc[4], pc[6], pc[7], head_g, w_four, w_mproj, w_out)
        new_ctx = h_ctx + gt_c * rmsnorm(out_c, g_post)
    return new_lat, new_ctx


def setup_inputs(seed: int = 0) -> dict:
    key = jax.random.key(seed)
    ks = jax.random.split(key, 20)
    nrm = lambda k, shape, s: jax.random.normal(k, shape, jnp.float32) * s
    D = D_MODEL
    b_in = nrm(ks[10], (DEPTH, IN_WIDTH), 0.02)
    f_bias = jnp.linspace(3.0, 6.0, M_HEADS, dtype=jnp.float32)
    b_in = b_in.at[:, COL_GATES + M_HEADS:COL_GATES + 2 * M_HEADS].add(f_bias)
    b_in = b_in.at[:, COL_GATES + 3 * M_HEADS:COL_GATES + 4 * M_HEADS].add(f_bias)
    return {
        "x": nrm(ks[0], (BATCH, SEQ, D), 1.0),
        "c": nrm(ks[1], (BATCH, D), 1.0),
        "ctx": nrm(ks[2], (BATCH, CTX_LEN, D), 1.0),
        "c_ctx": nrm(ks[3], (D,), 1.0),
        "w_ada": nrm(ks[4], (DEPTH, D, N_ADA * D), D ** -0.5),
        "b_ada": nrm(ks[5], (DEPTH, N_ADA * D), 0.02),
        "norm_g": 1.0 + nrm(ks[6], (DEPTH, 6, D), 0.05),
        "w13_a": nrm(ks[7], (DEPTH, D, 2 * FF_HALF), D ** -0.5),
        "w2_a": nrm(ks[8], (DEPTH, FF_HALF, D), FF_HALF ** -0.5),
        "w_in": nrm(ks[9], (DEPTH, D, IN_WIDTH), D ** -0.5),
        "b_in": b_in,
        "conv_w": nrm(ks[11], (DEPTH, CONV_K, 2 * M_WIDTH), CONV_K ** -0.5),
        "conv_b": nrm(ks[12], (DEPTH, 2 * M_WIDTH), 0.02),
        "head_g": 1.0 + nrm(ks[13], (DEPTH, M_WIDTH), 0.05),
        "w_four": nrm(ks[14], (DEPTH, F_WIDTH, D), F_WIDTH ** -0.5),
        "w_mproj": nrm(ks[15], (DEPTH, M_WIDTH, D), M_WIDTH ** -0.5),
        "w_out": nrm(ks[16], (DEPTH, D, D), D ** -0.5),
        "w13_b": nrm(ks[17], (DEPTH, D, 2 * FF_HALF), D ** -0.5),
        "w2_b": nrm(ks[18], (DEPTH, FF_HALF, D), FF_HALF ** -0.5),
    }


def reference(x, c, ctx, c_ctx, w_ada, b_ada, norm_g, w13_a, w2_a, w_in, b_in, conv_w, conv_b,
              head_g, w_four, w_mproj, w_out, w13_b, w2_b):
    h_lat = x
    h_ctx = ctx
    for l in range(DEPTH):
        last = l == DEPTH - 1
        ml = ada_params(c, w_ada[l], b_ada[l])
        mc = ada_params(c_ctx[None], w_ada[l], b_ada[l])
        g = norm_g[l]
        h_lat = half_ffn(h_lat, ml[:, 0], ml[:, 1], ml[:, 2], g[0], g[1], w13_a[l], w2_a[l])
        h_ctx = half_ffn(h_ctx, mc[:, 0], mc[:, 1], mc[:, 2], g[0], g[1], w13_a[l], w2_a[l])
        h_lat, h_ctx_new = token_mixing(
            h_lat, h_ctx, (ml[:, 3], ml[:, 4], ml[:, 5]), (mc[:, 3], mc[:, 4], mc[:, 5]),
            g[2], g[3], w_in[l], b_in[l], conv_w[l], conv_b[l], head_g[l],
            w_four[l], w_mproj[l], w_out[l], not last)
        h_lat = half_ffn(h_lat, ml[:, 6], ml[:, 7], ml[:, 8], g[4], g[5], w13_b[l], w2_b[l])
        if not last:
            h_ctx = half_ffn(h_ctx_new, mc[:, 6], mc[:, 7], mc[:, 8], g[4], g[5], w13_b[l], w2_b[l])
    return h_lat
```

```python
import functools
import math

import numpy as np
import jax
import jax.numpy as jnp
from jax import lax
from jax.experimental import pallas as pl
from jax.experimental.pallas import tpu as pltpu

D_MODEL = 1024
GRID_W = 64
FF_HALF = 2816
F_GROUPS = 4
F_GROUP_DIM = 128
F_WIDTH = F_GROUPS * F_GROUP_DIM
M_HEADS = 4
M_HEAD_DIM = 256
M_WIDTH = M_HEADS * M_HEAD_DIM
CONV_K = 3
CHUNK = 128
N_ADA = 9
EPS = 1e-6

COL_Q = F_WIDTH
COL_K = COL_Q + M_WIDTH
COL_V = COL_K + M_WIDTH
COL_O = COL_V + M_WIDTH
COL_GATES = COL_O + M_WIDTH
N_GATES = 4 * M_HEADS
COL_BR = COL_GATES + N_GATES

LANES = 128
BF16_SUBLANES = 16
MOD_ROWS = 8
HALO = BF16_SUBLANES
RCOL_GATES = COL_GATES
RCOL_BR = RCOL_GATES + LANES
R_WIDTH = RCOL_BR + 2 * D_MODEL
VMEM_LIMIT = 56 * 1024 * 1024

F32 = jnp.float32
BF16 = jnp.bfloat16


def _const_spec(shape):
    nd = len(shape)
    return pl.BlockSpec(shape, lambda *_: (0,) * nd, pipeline_mode=pl.Buffered(1))


def _params(*sem):
    return pltpu.CompilerParams(dimension_semantics=sem, vmem_limit_bytes=VMEM_LIMIT)


def _rms(x, g):
    return x * lax.rsqrt(jnp.mean(x * x, axis=-1, keepdims=True) + EPS) * g


def _dot(a, b):
    return jnp.dot(a, b, preferred_element_type=F32)


def _ada_kernel(c_ref, w_ref, b_ref, o_ref):
    c = c_ref[...]
    s = (c * jax.nn.sigmoid(c)).astype(BF16)
    o_ref[...] = _dot(s, w_ref[...].astype(BF16)) + b_ref[...]


def _ada(cvec, w, b):
    d = cvec.shape[1]
    out = pl.pallas_call(
        _ada_kernel,
        grid=(N_ADA,),
        in_specs=[
            pl.BlockSpec((MOD_ROWS, d), lambda j: (0, 0)),
            pl.BlockSpec((d, d), lambda j: (0, j)),
            pl.BlockSpec((1, d), lambda j: (0, j)),
        ],
        out_specs=pl.BlockSpec((MOD_ROWS, d), lambda j: (0, j)),
        out_shape=jax.ShapeDtypeStruct((MOD_ROWS, N_ADA * d), F32),
        compiler_params=_params("arbitrary"),
        name="ada",
    )(cvec, w, b)
    return out.reshape(MOD_ROWS, N_ADA, d)


FF_CHUNK = 256


def _ffn_kernel(x_ref, mod_ref, g_ref, w13_ref, w2_ref, o_ref, hid_ref, *, mod0, g0):
    x = x_ref[...]
    shift = mod_ref[mod0:mod0 + 1, :]
    scale = mod_ref[mod0 + 1:mod0 + 2, :]
    gate = mod_ref[mod0 + 2:mod0 + 3, :]
    u = (_rms(x, g_ref[g0:g0 + 1, :]) * (1.0 + scale) + shift).astype(BF16)
    for c in range(FF_HALF // FF_CHUNK):
        lo = c * FF_CHUNK
        a = _dot(u, w13_ref[:, lo:lo + FF_CHUNK])
        b = _dot(u, w13_ref[:, FF_HALF + lo:FF_HALF + lo + FF_CHUNK])
        hid_ref[:, lo:lo + FF_CHUNK] = (a * jax.nn.sigmoid(a) * b).astype(BF16)
    y = _dot(hid_ref[...], w2_ref[...])
    o_ref[...] = x + 0.5 * gate * _rms(y, g_ref[g0 + 1:g0 + 2, :])


def _ffn(h, mod, g, w13, w2, *, tm, tiles_per_seq, mod_row0, mod_step, mod0, g0, name):
    n, d = h.shape
    kern = functools.partial(_ffn_kernel, mod0=mod0, g0=g0)
    return pl.pallas_call(
        kern,
        grid=(n // tm,),
        in_specs=[
            pl.BlockSpec((tm, d), lambda i: (i, 0)),
            pl.BlockSpec((None, N_ADA, d), lambda i: (mod_row0 + mod_step * (i // tiles_per_seq), 0, 0)),
            _const_spec(g.shape),
            _const_spec(w13.shape),
            _const_spec(w2.shape),
        ],
        out_specs=pl.BlockSpec((tm, d), lambda i: (i, 0)),
        out_shape=jax.ShapeDtypeStruct((n, d), F32),
        scratch_shapes=[pltpu.VMEM((tm, FF_HALF), BF16)],
        compiler_params=_params("parallel"),
        name=name,
    )(h, mod, g, w13, w2)


QK_CHUNK = 256


def _proj_kernel(x_ref, xp_ref, xn_ref, mod_ref, g_ref, w_ref, b_ref, cw_ref, cb_ref,
                 xf_ref, q_ref, kt_ref, v_ref, o_ref, gt_ref, gf_ref, gm_ref, u_ref,
                 *, tm, tiles_per_seq):
    i = pl.program_id(0)
    first = (i % tiles_per_seq) == 0
    last = (i % tiles_per_seq) == tiles_per_seq - 1
    shift = mod_ref[3:4, :]
    scale = mod_ref[4:5, :]
    g = g_ref[2:3, :]

    def mk_u(x):
        return (_rms(x, g) * (1.0 + scale) + shift).astype(BF16)

    u_ref[0:HALO, :] = mk_u(xp_ref[...])
    u_ref[HALO:HALO + tm, :] = mk_u(x_ref[...])
    u_ref[HALO + tm:, :] = mk_u(xn_ref[...])
    u = u_ref[HALO:HALO + tm, :]

    def lin(lo, width):
        return _dot(u, w_ref[:, lo:lo + width]) + b_ref[:, lo:lo + width]

    xf_ref[...] = lin(0, F_WIDTH).astype(BF16)
    v_ref[...] = lin(COL_V, M_WIDTH).astype(BF16)
    o_ref[...] = lin(COL_O, M_WIDTH)
    gt_ref[...] = lin(RCOL_GATES, LANES).T[0:N_GATES, :]
    gf_ref[...] = lin(RCOL_BR, D_MODEL)
    gm_ref[...] = lin(RCOL_BR + D_MODEL, D_MODEL)

    row = lax.broadcasted_iota(jnp.int32, (tm + 2 * HALO, 1), 0)
    lo_valid = jnp.where(first, HALO, 0)
    hi_valid = jnp.where(last, HALO + tm, tm + 2 * HALO)
    valid = jnp.logical_and(row >= lo_valid, row < hi_valid)
    u_ext = u_ref[...]
    for c in range(2 * M_WIDTH // QK_CHUNK):
        lo = c * QK_CHUNK
        p = _dot(u_ext, w_ref[:, COL_Q + lo:COL_Q + lo + QK_CHUNK]) + b_ref[:, COL_Q + lo:COL_Q + lo + QK_CHUNK]
        p = jnp.where(valid, p, 0.0)
        cw = cw_ref[:, lo:lo + QK_CHUNK]
        y = (cw[0:1, :] * pltpu.roll(p, 1, 0) + cw[1:2, :] * p
             + cw[2:3, :] * pltpu.roll(p, tm + 2 * HALO - 1, 0))
        y = y[HALO:HALO + tm, :] + cb_ref[:, lo:lo + QK_CHUNK]
        y = y * jax.nn.sigmoid(y)
        if lo < M_WIDTH:
            q_ref[:, lo:lo + QK_CHUNK] = y.astype(BF16)
        else:
            kt_ref[lo - M_WIDTH:lo - M_WIDTH + QK_CHUNK, :] = (y * (M_HEAD_DIM ** -0.5)).T.astype(BF16)


def _proj(h, mod, g, w, b, cw, cb, *, tm, tiles_per_seq, mod_row0, mod_step, name):
    n, d = h.shape
    hb = tm // HALO
    nhb = n // HALO
    kern = functools.partial(_proj_kernel, tm=tm, tiles_per_seq=tiles_per_seq)
    tok = lambda width: pl.BlockSpec((tm, width), lambda i: (i, 0))
    outs = pl.pallas_call(
        kern,
        grid=(n // tm,),
        in_specs=[
            pl.BlockSpec((tm, d), lambda i: (i, 0)),
            pl.BlockSpec((HALO, d), lambda i: (jnp.maximum(i * hb - 1, 0), 0)),
            pl.BlockSpec((HALO, d), lambda i: (jnp.minimum((i + 1) * hb, nhb - 1), 0)),
            pl.BlockSpec((None, N_ADA, d), lambda i: (mod_row0 + mod_step * (i // tiles_per_seq), 0, 0)),
            _const_spec(g.shape),
            _const_spec(w.shape),
            _const_spec(b.shape),
            _const_spec(cw.shape),
            _const_spec(cb.shape),
        ],
        out_specs=[
            tok(F_WIDTH), tok(M_WIDTH),
            pl.BlockSpec((M_WIDTH, tm), lambda i: (0, i)),
            tok(M_WIDTH), tok(M_WIDTH),
            pl.BlockSpec((N_GATES, tm), lambda i: (0, i)),
            tok(D_MODEL), tok(D_MODEL),
        ],
        out_shape=[
            jax.ShapeDtypeStruct((n, F_WIDTH), BF16),
            jax.ShapeDtypeStruct((n, M_WIDTH), BF16),
            jax.ShapeDtypeStruct((M_WIDTH, n), BF16),
            jax.ShapeDtypeStruct((n, M_WIDTH), BF16),
            jax.ShapeDtypeStruct((n, M_WIDTH), F32),
            jax.ShapeDtypeStruct((N_GATES, n), F32),
            jax.ShapeDtypeStruct((n, D_MODEL), F32),
            jax.ShapeDtypeStruct((n, D_MODEL), F32),
        ],
        scratch_shapes=[pltpu.VMEM((tm + 2 * HALO, d), BF16)],
        compiler_params=_params("parallel"),
        name=name,
    )(h, h, h, mod, g, w, b, cw, cb)
    return outs


def _lane_scan(x, reverse):
    lane = lax.broadcasted_iota(jnp.int32, x.shape, 1)
    k = 1
    while k < CHUNK:
        if reverse:
            x = x + jnp.where(lane < CHUNK - k, pltpu.roll(x, CHUNK - k, 1), 0.0)
        else:
            x = x + jnp.where(lane >= k, pltpu.roll(x, k, 1), 0.0)
        k *= 2
    return x


def _mlstm_kernel(q_ref, kt_ref, v_ref, ktc_ref, vc_ref, gl_ref, gc_ref, o_ref,
                  c_ref, n_ref, r_ref, wk_ref, lf_ref, ms_ref, dec_ref, *, ncc, ncl):
    nc = ncc + ncl
    L = CHUNK
    ones_rhs = jnp.ones((L, LANES), BF16)
    t_idx = lax.broadcasted_iota(jnp.int32, (L, L), 0)
    s_idx = lax.broadcasted_iota(jnp.int32, (L, L), 1)
    masks = (s_idx <= t_idx, s_idx >= t_idx)

    def gate_rows(g_ref, d):
        li = g_ref[2 * d]
        gf = g_ref[2 * d + 1]
        lf = jnp.minimum(gf, 0.0) - jnp.log1p(jnp.exp(-jnp.abs(gf)))
        bcum = _lane_scan(lf, reverse=(d == 1))
        r = li - bcum
        rmax = jnp.max(r, axis=-1, keepdims=True)
        b_end = bcum[:, L - 1:L] if d == 0 else bcum[:, 0:1]
        return lf, r, rmax, b_end

    for d in range(2):
        parts = (gate_rows(gl_ref, d), gate_rows(gc_ref, d))
        rev = d == 1
        order = [(1, c) for c in (range(ncc - 1, -1, -1) if rev else range(ncc))]
        order += [(0, c) for c in (range(ncl - 1, -1, -1) if rev else range(ncl))]
        m = jnp.zeros((1, 1), F32)
        for part, c in order:
            row = c + (ncl if part == 1 else 0)
            ms_ref[d, row:row + 1, :] = jnp.broadcast_to(m, (1, LANES))
            _, _, rmax, b_end = parts[part]
            m = b_end[c:c + 1, :] + jnp.maximum(m, rmax[c:c + 1, :])
        for part, (lo, cnt) in enumerate(((0, ncl), (ncl, ncc))):
            lf, r, rmax, _ = parts[part]
            m_start = ms_ref[d, lo:lo + cnt, 0:1]
            m_end = jnp.maximum(m_start, rmax)
            dec_ref[d, lo:lo + cnt, :] = jnp.broadcast_to(jnp.exp(m_start - m_end), (cnt, LANES))
            wk_ref[d, lo:lo + cnt, :] = jnp.exp(r - m_end)
            r_ref[d, lo:lo + cnt, :] = r
            lf_ref[d, lo:lo + cnt, :] = lf

    c_ref[...] = jnp.zeros_like(c_ref)
    n_ref[...] = jnp.zeros_like(n_ref)

    def update_state(d, row, kt_c, v_c):
        wk = wk_ref[d, pl.ds(row, 1), :]
        dec = dec_ref[d, pl.ds(row, 1), :]
        ktw = (kt_c.astype(F32) * wk).astype(BF16)
        c_ref[d] = dec[:, 0:1] * c_ref[d] + _dot(ktw, v_c)
        n_ref[d] = dec * n_ref[d] + _dot(ktw, ones_rhs)

    def chunk_out(d, row, q_c, kt_c, v_c):
        r_row = r_ref[d, pl.ds(row, 1), :]
        lf_row = lf_ref[d, pl.ds(row, 1), :]
        m = ms_ref[d, pl.ds(row, 1), :][:, 0:1]
        rm = jnp.where(masks[d], r_row, -jnp.inf)
        mt = jnp.maximum(m, jnp.max(rm, axis=-1, keepdims=True))
        bc = jnp.sum(jnp.where(masks[d], lf_row, 0.0), axis=-1, keepdims=True)
        w = jnp.exp(rm - mt)
        a = jnp.exp(m - mt)
        e = jnp.exp(-bc - mt)
        s = _dot(q_c, kt_c) * w
        qc = _dot(q_c, c_ref[d].astype(BF16))
        qn = _dot(q_c, n_ref[d].astype(BF16))
        num = a * qc + _dot(s.astype(BF16), v_c)
        den = a * qn[:, 0:1] + jnp.sum(s, axis=-1, keepdims=True)
        return num / jnp.maximum(jnp.abs(den), e)

    for d in range(2):
        for c in (range(ncc) if d == 0 else range(ncc - 1, -1, -1)):
            update_state(d, ncl + c, ktc_ref[:, c * L:(c + 1) * L], vc_ref[c * L:(c + 1) * L, :])

    def lat_step(j, accumulate):
        for d in range(2):
            c = j if d == 0 else ncl - 1 - j
            off = pl.multiple_of(c * L, L)
            q_c = q_ref[pl.ds(off, L), :]
            kt_c = kt_ref[:, pl.ds(off, L)]
            v_c = v_ref[pl.ds(off, L), :]
            h = chunk_out(d, c, q_c, kt_c, v_c)
            if accumulate:
                o_ref[pl.ds(off, L), :] += h
            else:
                o_ref[pl.ds(off, L), :] = h
            update_state(d, c, kt_c, v_c)

    def first_half(j, carry):
        lat_step(j, False)
        return carry

    def second_half(j, carry):
        lat_step(j, True)
        return carry

    lax.fori_loop(0, ncl // 2, first_half, 0)
    lax.fori_loop(ncl // 2, ncl, second_half, 0)


def _mlstm(q, kt, v, ktc, vc, gl, gc, *, batch, t_lat, t_ctx):
    ncl = t_lat // CHUNK
    ncc = t_ctx // CHUNK
    nc = ncc + ncl
    assert ncl % 2 == 0
    dh = M_HEAD_DIM
    kern = functools.partial(_mlstm_kernel, ncc=ncc, ncl=ncl)
    return pl.pallas_call(
        kern,
        grid=(batch, M_HEADS),
        in_specs=[
            pl.BlockSpec((t_lat, dh), lambda b, h: (b, h)),
            pl.BlockSpec((dh, t_lat), lambda b, h: (h, b)),
            pl.BlockSpec((t_lat, dh), lambda b, h: (b, h)),
            pl.BlockSpec((dh, t_ctx), lambda b, h: (h, b)),
            pl.BlockSpec((t_ctx, dh), lambda b, h: (b, h)),
            pl.BlockSpec((4, None, None, ncl, CHUNK), lambda b, h: (0, h, b, 0, 0)),
            pl.BlockSpec((4, None, None, ncc, CHUNK), lambda b, h: (0, h, b, 0, 0)),
        ],
        out_specs=pl.BlockSpec((t_lat, dh), lambda b, h: (b, h)),
        out_shape=jax.ShapeDtypeStruct((batch * t_lat, M_WIDTH), F32),
        scratch_shapes=[
            pltpu.VMEM((2, dh, dh), F32),
            pltpu.VMEM((2, dh, LANES), F32),
            pltpu.VMEM((2, nc, CHUNK), F32),
            pltpu.VMEM((2, nc, CHUNK), F32),
            pltpu.VMEM((2, nc, CHUNK), F32),
            pltpu.VMEM((2, nc, LANES), F32),
            pltpu.VMEM((2, nc, LANES), F32),
        ],
        compiler_params=_params("parallel", "arbitrary"),
        name="mlstm",
    )(q, kt, v, ktc, vc, gl, gc)


S1_ROWS = 512


def _fourier_kernel(x_ref, wk_ref, cs_ref, cr_ref, sr_ref, o_ref, x1_ref, x2r_ref, x2i_ref, *, rows, scale):
    t = rows * GRID_W
    gd = F_GROUP_DIM
    n = min(S1_ROWS, t)
    for c in range(t // n):
        x1_ref[c * n:(c + 1) * n, :] = _dot(x_ref[c * n:(c + 1) * n, :], wk_ref[...]).astype(BF16)

    def col_step(r, carry):
        x1 = x1_ref[pl.ds(pl.multiple_of(r * GRID_W, GRID_W), GRID_W), :]
        pq = _dot(cs_ref[...], x1)
        p = pq[0:GRID_W, :]
        q = pq[GRID_W:, :]
        x2r_ref[pl.ds(r, GRID_W, stride=rows), :] = p[:, 0:gd] + q[:, gd:]
        x2i_ref[pl.ds(r, GRID_W, stride=rows), :] = p[:, gd:] - q[:, 0:gd]
        return carry

    lax.fori_loop(0, rows, col_step, 0)

    def row_step(c, carry):
        sl = pl.ds(pl.multiple_of(c * rows, rows), rows)
        y = (_dot(cr_ref[...], x2r_ref[sl, :].astype(BF16))
             + _dot(sr_ref[...], x2i_ref[sl, :].astype(BF16)))
        o_ref[pl.ds(c, rows, stride=GRID_W), :] = y * scale
        return carry

    lax.fori_loop(0, GRID_W, row_step, 0)


def _dft_mats(n):
    k = np.arange(n)
    ang = 2.0 * np.pi * np.outer(k, k) / n
    return np.cos(ang), np.sin(ang)


def _fourier(xf, *, batch, t_lat):
    rows = t_lat // GRID_W
    ck, sk = _dft_mats(F_GROUP_DIM)
    cc, sc = _dft_mats(GRID_W)
    cr, sr = _dft_mats(rows)
    wk = jnp.asarray(np.concatenate([ck, -sk], axis=1), F32).astype(BF16)
    cs = jnp.asarray(np.concatenate([cc, sc], axis=0), F32).astype(BF16)
    crj = jnp.asarray(cr, F32).astype(BF16)
    srj = jnp.asarray(sr, F32).astype(BF16)
    scale = 1.0 / math.sqrt(rows * GRID_W * F_GROUP_DIM)
    kern = functools.partial(_fourier_kernel, rows=rows, scale=scale)
    gd = F_GROUP_DIM
    return pl.pallas_call(
        kern,
        grid=(batch, F_GROUPS),
        in_specs=[
            pl.BlockSpec((t_lat, gd), lambda b, g: (b, g)),
            _const_spec(wk.shape), _const_spec(cs.shape), _const_spec(crj.shape), _const_spec(srj.shape),
        ],
        out_specs=pl.BlockSpec((t_lat, gd), lambda b, g: (b, g)),
        out_shape=jax.ShapeDtypeStruct((batch * t_lat, F_WIDTH), F32),
        scratch_shapes=[
            pltpu.VMEM((t_lat, 2 * gd), BF16),
            pltpu.VMEM((t_lat, gd), F32),
            pltpu.VMEM((t_lat, gd), F32),
        ],
        compiler_params=_params("parallel", "parallel"),
        name="fourier",
    )(xf, wk, cs, crj, srj)


def _merge_kernel(x_ref, yf_ref, h_ref, o_ref, gf_ref, gm_ref, mod_ref, g_ref, hg_ref,
                  wf_ref, wm_ref, wo_ref, out_ref, hm_ref):
    dh = M_HEAD_DIM
    for hd in range(M_HEADS):
        sl = slice(hd * dh, (hd + 1) * dh)
        hn = _rms(h_ref[:, sl], hg_ref[:, sl])
        hm_ref[:, sl] = (jax.nn.sigmoid(o_ref[:, sl]) * hn).astype(BF16)
    y = (jax.nn.sigmoid(gf_ref[...]) * _dot(yf_ref[...].astype(BF16), wf_ref[...])
         + jax.nn.sigmoid(gm_ref[...]) * _dot(hm_ref[...], wm_ref[...]))
    out = _dot(y.astype(BF16), wo_ref[...])
    out_ref[...] = x_ref[...] + mod_ref[5:6, :] * _rms(out, g_ref[3:4, :])


def _merge(x, yf, h, o, gf, gm, mod, g, hg, wf, wm, wo, *, tm, tiles_per_seq):
    n, d = x.shape
    tok = lambda width: pl.BlockSpec((tm, width), lambda i: (i, 0))
    return pl.pallas_call(
        _merge_kernel,
        grid=(n // tm,),
        in_specs=[
            tok(d), tok(F_WIDTH), tok(M_WIDTH), tok(M_WIDTH), tok(d), tok(d),
            pl.BlockSpec((None, N_ADA, d), lambda i: (i // tiles_per_seq, 0, 0)),
            _const_spec(g.shape), _const_spec(hg.shape),
            _const_spec(wf.shape), _const_spec(wm.shape), _const_spec(wo.shape),
        ],
        out_specs=tok(d),
        out_shape=jax.ShapeDtypeStruct((n, d), F32),
        scratch_shapes=[pltpu.VMEM((tm, M_WIDTH), BF16)],
        compiler_params=_params("parallel"),
        name="merge",
    )(x, yf, h, o, gf, gm, mod, g, hg, wf, wm, wo)


def _tile(t, cap):
    tm = min(t, cap)
    assert t % tm == 0
    return tm


def kernel(x, c, ctx, c_ctx, w_ada, b_ada, norm_g, w13_a, w2_a, w_in, b_in, conv_w, conv_b,
           head_g, w_four, w_mproj, w_out, w13_b, w2_b):
    batch, t_lat, d = x.shape
    t_ctx = ctx.shape[1]
    assert d == D_MODEL and w_ada.shape[0] == 1, "single-layer kernel"
    assert batch + 1 <= MOD_ROWS and t_lat % (GRID_W * CHUNK // math.gcd(GRID_W, CHUNK)) == 0
    assert t_ctx % CHUNK == 0

    cvec = jnp.concatenate([c, c_ctx[None], jnp.zeros((MOD_ROWS - batch - 1, d), F32)], axis=0)
    mod = _ada(cvec, w_ada[0], b_ada[0][None])
    g = norm_g[0]

    gpad = LANES - N_GATES
    w_r = jnp.concatenate([w_in[0][:, :COL_GATES], jnp.pad(w_in[0][:, COL_GATES:COL_BR], ((0, 0), (0, gpad))),
                           w_in[0][:, COL_BR:]], axis=1).astype(BF16)
    b_r = jnp.concatenate([b_in[0][:COL_GATES], jnp.pad(b_in[0][COL_GATES:COL_BR], (0, gpad)),
                           b_in[0][COL_BR:]])[None]
    w13a, w2a = w13_a[0].astype(BF16), w2_a[0].astype(BF16)
    w13b, w2b = w13_b[0].astype(BF16), w2_b[0].astype(BF16)
    wf, wm, wo = w_four[0].astype(BF16), w_mproj[0].astype(BF16), w_out[0].astype(BF16)
    cw, cb, hg = conv_w[0], conv_b[0][None], head_g[0][None]

    tm_l = _tile(t_lat, 512)
    tm_c = _tile(t_ctx, 512)
    tps_l = t_lat // tm_l
    tps_c = t_ctx // tm_c
    xl = x.reshape(batch * t_lat, d)
    xc = ctx.reshape(batch * t_ctx, d)

    lat = dict(tm=tm_l, tiles_per_seq=tps_l, mod_row0=0, mod_step=1)
    con = dict(tm=tm_c, tiles_per_seq=tps_c, mod_row0=batch, mod_step=0)

    hl = _ffn(xl, mod, g, w13a, w2a, mod0=0, g0=0, name="ffn_a_lat", **lat)
    hc = _ffn(xc, mod, g, w13a, w2a, mod0=0, g0=0, name="ffn_a_ctx", **con)

    xf, q, kt, v, o, gt, gf, gm = _proj(hl, mod, g, w_r, b_r, cw, cb, name="proj_lat", **lat)
    _, _, ktc, vc, _, gtc, _, _ = _proj(hc, mod, g, w_r, b_r, cw, cb, name="proj_ctx", **con)

    gl = gt.reshape(4, M_HEADS, batch, t_lat // CHUNK, CHUNK)
    gc = gtc.reshape(4, M_HEADS, batch, t_ctx // CHUNK, CHUNK)
    hm = _mlstm(q, kt, v, ktc, vc, gl, gc, batch=batch, t_lat=t_lat, t_ctx=t_ctx)
    yf = _fourier(xf, batch=batch, t_lat=t_lat)

    hl = _merge(hl, yf, hm, o, gf, gm, mod, g, hg, wf, wm, wo, tm=tm_l, tiles_per_seq=tps_l)
    hl = _ffn(hl, mod, g, w13b, w2b, mod0=6, g0=4, name="ffn_b_lat", **lat)
    return hl.reshape(batch, t_lat, d)
```

```python
import functools
import math

import numpy as np
import jax
import jax.numpy as jnp
from jax import lax
from jax.experimental import pallas as pl
from jax.experimental.pallas import tpu as pltpu

D_MODEL = 1024
GRID_W = 64
FF_HALF = 2816
F_GROUPS = 4
F_GROUP_DIM = 128
F_WIDTH = F_GROUPS * F_GROUP_DIM
M_HEADS = 4
M_HEAD_DIM = 256
M_WIDTH = M_HEADS * M_HEAD_DIM
CONV_K = 3
CHUNK = 128
N_ADA = 9
EPS = 1e-6

COL_Q = F_WIDTH
COL_K = COL_Q + M_WIDTH
COL_V = COL_K + M_WIDTH
COL_O = COL_V + M_WIDTH
COL_GATES = COL_O + M_WIDTH
N_GATES = 4 * M_HEADS
COL_BR = COL_GATES + N_GATES

LANES = 128
BF16_SUBLANES = 16
MOD_ROWS = 8
HALO = BF16_SUBLANES
VMEM_LIMIT = 56 * 1024 * 1024

F32 = jnp.float32
BF16 = jnp.bfloat16


def _const_spec(shape):
    nd = len(shape)
    return pl.BlockSpec(shape, lambda *_: (0,) * nd, pipeline_mode=pl.Buffered(1))


def _params(*sem):
    return pltpu.CompilerParams(dimension_semantics=sem, vmem_limit_bytes=VMEM_LIMIT)


def _rms(x, g):
    return x * lax.rsqrt(jnp.mean(x * x, axis=-1, keepdims=True) + EPS) * g


def _dot(a, b):
    return jnp.dot(a, b, preferred_element_type=F32)


def _ada_kernel(c_ref, w_ref, b_ref, o_ref):
    c = c_ref[...]
    s = (c * jax.nn.sigmoid(c)).astype(BF16)
    o_ref[...] = _dot(s, w_ref[...].astype(BF16)) + b_ref[...]


def _ada(cvec, w, b):
    d = cvec.shape[1]
    out = pl.pallas_call(
        _ada_kernel,
        grid=(N_ADA,),
        in_specs=[
            pl.BlockSpec((MOD_ROWS, d), lambda j: (0, 0)),
            pl.BlockSpec((d, d), lambda j: (0, j)),
            pl.BlockSpec((1, d), lambda j: (0, j)),
        ],
        out_specs=pl.BlockSpec((MOD_ROWS, d), lambda j: (0, j)),
        out_shape=jax.ShapeDtypeStruct((MOD_ROWS, N_ADA * d), F32),
        compiler_params=_params("arbitrary"),
        name="ada",
    )(cvec, w, b)
    return out.reshape(MOD_ROWS, N_ADA, d)


FF_CHUNK = 256


def _ffn_kernel(x_ref, mod_ref, g_ref, w13_ref, w2_ref, o_ref, hid_ref, *, mod0, g0):
    x = x_ref[...]
    shift = mod_ref[mod0:mod0 + 1, :]
    scale = mod_ref[mod0 + 1:mod0 + 2, :]
    gate = mod_ref[mod0 + 2:mod0 + 3, :]
    u = (_rms(x, g_ref[g0:g0 + 1, :]) * (1.0 + scale) + shift).astype(BF16)
    for c in range(FF_HALF // FF_CHUNK):
        lo = c * FF_CHUNK
        a = _dot(u, w13_ref[:, lo:lo + FF_CHUNK])
        b = _dot(u, w13_ref[:, FF_HALF + lo:FF_HALF + lo + FF_CHUNK])
        hid_ref[:, lo:lo + FF_CHUNK] = (a * jax.nn.sigmoid(a) * b).astype(BF16)
    y = _dot(hid_ref[...], w2_ref[...])
    o_ref[...] = x + 0.5 * gate * _rms(y, g_ref[g0 + 1:g0 + 2, :])


def _ffn(h, mod, g, w13, w2, *, tm, tiles_per_seq, mod_row0, mod_step, mod0, g0, name):
    n, d = h.shape
    kern = functools.partial(_ffn_kernel, mod0=mod0, g0=g0)
    return pl.pallas_call(
        kern,
        grid=(n // tm,),
        in_specs=[
            pl.BlockSpec((tm, d), lambda i: (i, 0)),
            pl.BlockSpec((None, N_ADA, d), lambda i: (mod_row0 + mod_step * (i // tiles_per_seq), 0, 0)),
            _const_spec(g.shape),
            _const_spec(w13.shape),
            _const_spec(w2.shape),
        ],
        out_specs=pl.BlockSpec((tm, d), lambda i: (i, 0)),
        out_shape=jax.ShapeDtypeStruct((n, d), F32),
        scratch_shapes=[pltpu.VMEM((tm, FF_HALF), BF16)],
        compiler_params=_params("parallel"),
        name=name,
    )(h, mod, g, w13, w2)


QK_CHUNK = 256


def _proj_kernel(x_ref, xp_ref, xn_ref, mod_ref, g_ref, w_ref, wg_ref, wbr_ref, b_ref, bg_ref, bbr_ref,
                 cw_ref, cb_ref, *refs, tm, tiles_per_seq, full):
    if full:
        xf_ref, q_ref, kt_ref, v_ref, o_ref, gt_ref, gf_ref, gm_ref, u_ref = refs
    else:
        kt_ref, v_ref, gt_ref, u_ref = refs
    i = pl.program_id(0)
    first = (i % tiles_per_seq) == 0
    last = (i % tiles_per_seq) == tiles_per_seq - 1
    shift = mod_ref[3:4, :]
    scale = mod_ref[4:5, :]
    g = g_ref[2:3, :]

    def mk_u(x):
        return (_rms(x, g) * (1.0 + scale) + shift).astype(BF16)

    u_ref[0:HALO, :] = mk_u(xp_ref[...])
    u_ref[HALO:HALO + tm, :] = mk_u(x_ref[...])
    u_ref[HALO + tm:, :] = mk_u(xn_ref[...])
    u = u_ref[HALO:HALO + tm, :]

    def lin(lo, width):
        return _dot(u, w_ref[:, lo:lo + width]) + b_ref[:, lo:lo + width]

    v_ref[...] = lin(COL_V, M_WIDTH).astype(BF16)
    gt_ref[...] = (_dot(u, wg_ref[...]) + bg_ref[...]).T[0:N_GATES, :]
    if full:
        xf_ref[...] = lin(0, F_WIDTH).astype(BF16)
        o_ref[...] = lin(COL_O, M_WIDTH).astype(BF16)
        gf_ref[...] = (_dot(u, wbr_ref[:, 0:D_MODEL]) + bbr_ref[:, 0:D_MODEL]).astype(BF16)
        gm_ref[...] = (_dot(u, wbr_ref[:, D_MODEL:]) + bbr_ref[:, D_MODEL:]).astype(BF16)

    row = lax.broadcasted_iota(jnp.int32, (tm + 2 * HALO, 1), 0)
    lo_valid = jnp.where(first, HALO, 0)
    hi_valid = jnp.where(last, HALO + tm, tm + 2 * HALO)
    valid = jnp.logical_and(row >= lo_valid, row < hi_valid)
    u_ext = u_ref[...]
    for c in range(0 if full else M_WIDTH // QK_CHUNK, 2 * M_WIDTH // QK_CHUNK):
        lo = c * QK_CHUNK
        p = _dot(u_ext, w_ref[:, COL_Q + lo:COL_Q + lo + QK_CHUNK]) + b_ref[:, COL_Q + lo:COL_Q + lo + QK_CHUNK]
        p = jnp.where(valid, p, 0.0)
        cw = cw_ref[:, lo:lo + QK_CHUNK]
        y = (cw[0:1, :] * pltpu.roll(p, 1, 0) + cw[1:2, :] * p
             + cw[2:3, :] * pltpu.roll(p, tm + 2 * HALO - 1, 0))
        y = y[HALO:HALO + tm, :] + cb_ref[:, lo:lo + QK_CHUNK]
        y = y * jax.nn.sigmoid(y)
        if lo < M_WIDTH:
            q_ref[:, lo:lo + QK_CHUNK] = y.astype(BF16)
        else:
            kt_ref[lo - M_WIDTH:lo - M_WIDTH + QK_CHUNK, :] = (y * (M_HEAD_DIM ** -0.5)).T.astype(BF16)


def _proj(h, mod, g, ws, bs, cw, cb, *, tm, tiles_per_seq, mod_row0, mod_step, full, name):
    n, d = h.shape
    hb = tm // HALO
    nhb = n // HALO
    kern = functools.partial(_proj_kernel, tm=tm, tiles_per_seq=tiles_per_seq, full=full)
    tok = lambda width, dt: (pl.BlockSpec((tm, width), lambda i: (i, 0)), jax.ShapeDtypeStruct((n, width), dt))
    tr = lambda width, dt: (pl.BlockSpec((width, tm), lambda i: (0, i)), jax.ShapeDtypeStruct((width, n), dt))
    xf, q, o, gf, gm = tok(F_WIDTH, BF16), tok(M_WIDTH, BF16), tok(M_WIDTH, BF16), tok(d, BF16), tok(d, BF16)
    kt = tr(M_WIDTH, BF16)
    v = tok(M_WIDTH, BF16)
    gt = tr(N_GATES, F32)
    outs = (xf, q, kt, v, o, gt, gf, gm) if full else (kt, v, gt)
    return pl.pallas_call(
        kern,
        grid=(n // tm,),
        in_specs=[
            pl.BlockSpec((tm, d), lambda i: (i, 0)),
            pl.BlockSpec((HALO, d), lambda i: (jnp.maximum(i * hb - 1, 0), 0)),
            pl.BlockSpec((HALO, d), lambda i: (jnp.minimum((i + 1) * hb, nhb - 1), 0)),
            pl.BlockSpec((None, N_ADA, d), lambda i: (mod_row0 + mod_step * (i // tiles_per_seq), 0, 0)),
            _const_spec(g.shape),
            *[_const_spec(a.shape) for a in (*ws, *bs, cw, cb)],
        ],
        out_specs=[s for s, _ in outs],
        out_shape=[s for _, s in outs],
        scratch_shapes=[pltpu.VMEM((tm + 2 * HALO, d), BF16)],
        compiler_params=_params("parallel"),
        name=name,
    )(h, h, h, mod, g, *ws, *bs, cw, cb)


def _lane_scan(x, reverse):
    lane = lax.broadcasted_iota(jnp.int32, x.shape, 1)
    k = 1
    while k < CHUNK:
        if reverse:
            x = x + jnp.where(lane < CHUNK - k, pltpu.roll(x, CHUNK - k, 1), 0.0)
        else:
            x = x + jnp.where(lane >= k, pltpu.roll(x, k, 1), 0.0)
        k *= 2
    return x


def _mlstm_kernel(q_ref, kt_ref, v_ref, ktc_ref, vc_ref, gl_ref, gc_ref, o_ref,
                  c_ref, n_ref, r_ref, wk_ref, lf_ref, ms_ref, dec_ref, *, ncc, ncl):
    nc = ncc + ncl
    L = CHUNK
    ones_rhs = jnp.ones((L, LANES), BF16)
    t_idx = lax.broadcasted_iota(jnp.int32, (L, L), 0)
    s_idx = lax.broadcasted_iota(jnp.int32, (L, L), 1)
    masks = (s_idx <= t_idx, s_idx >= t_idx)

    def gate_rows(g_ref, d):
        li = g_ref[2 * d]
        gf = g_ref[2 * d + 1]
        lf = jnp.minimum(gf, 0.0) - jnp.log1p(jnp.exp(-jnp.abs(gf)))
        bcum = _lane_scan(lf, reverse=(d == 1))
        r = li - bcum
        rmax = jnp.max(r, axis=-1, keepdims=True)
        b_end = bcum[:, L - 1:L] if d == 0 else bcum[:, 0:1]
        return lf, r, rmax, b_end

    for d in range(2):
        parts = (gate_rows(gl_ref, d), gate_rows(gc_ref, d))
        rev = d == 1
        order = [(1, c) for c in (range(ncc - 1, -1, -1) if rev else range(ncc))]
        order += [(0, c) for c in (range(ncl - 1, -1, -1) if rev else range(ncl))]
        m = jnp.zeros((1, 1), F32)
        for part, c in order:
            row = c + (ncl if part == 1 else 0)
            ms_ref[d, row:row + 1, :] = jnp.broadcast_to(m, (1, LANES))
            _, _, rmax, b_end = parts[part]
            m = b_end[c:c + 1, :] + jnp.maximum(m, rmax[c:c + 1, :])
        for part, (lo, cnt) in enumerate(((0, ncl), (ncl, ncc))):
            lf, r, rmax, _ = parts[part]
            m_start = ms_ref[d, lo:lo + cnt, 0:1]
            m_end = jnp.maximum(m_start, rmax)
            dec_ref[d, lo:lo + cnt, :] = jnp.broadcast_to(jnp.exp(m_start - m_end), (cnt, LANES))
            wk_ref[d, lo:lo + cnt, :] = jnp.exp(r - m_end)
            r_ref[d, lo:lo + cnt, :] = r
            lf_ref[d, lo:lo + cnt, :] = lf

    c_ref[...] = jnp.zeros_like(c_ref)
    n_ref[...] = jnp.zeros_like(n_ref)

    def update_state(d, row, kt_c, v_c):
        wk = wk_ref[d, pl.ds(row, 1), :]
        dec = dec_ref[d, pl.ds(row, 1), :]
        ktw = (kt_c.astype(F32) * wk).astype(BF16)
        c_ref[d] = dec[:, 0:1] * c_ref[d] + _dot(ktw, v_c)
        n_ref[d] = dec * n_ref[d] + _dot(ktw, ones_rhs)

    def chunk_out(d, row, q_c, kt_c, v_c):
        r_row = r_ref[d, pl.ds(row, 1), :]
        lf_row = lf_ref[d, pl.ds(row, 1), :]
        m = ms_ref[d, pl.ds(row, 1), :]
        rm = jnp.where(masks[d], r_row, -jnp.inf)
        mt = jnp.maximum(m, jnp.max(rm, axis=-1, keepdims=True))
        bc = jnp.sum(jnp.where(masks[d], lf_row, 0.0), axis=-1, keepdims=True)
        w = jnp.exp(rm - mt)
        a = jnp.exp(m - mt)
        e = jnp.exp(-bc - mt)
        sqn = _dot(q_c, jnp.concatenate([kt_c, n_ref[d].astype(BF16)], axis=1))
        s = sqn[:, 0:L] * w
        qc = _dot(q_c, c_ref[d].astype(BF16))
        sv = _dot(s.astype(BF16), v_c)
        den = a * sqn[:, L:] + jnp.sum(s, axis=-1, keepdims=True)
        inv = 1.0 / jnp.maximum(jnp.abs(den), e)
        halves = [(a * qc[:, k:k + LANES] + sv[:, k:k + LANES]) * inv for k in range(0, M_HEAD_DIM, LANES)]
        return jnp.concatenate(halves, axis=1)

    for d in range(2):
        for c in (range(ncc) if d == 0 else range(ncc - 1, -1, -1)):
            update_state(d, ncl + c, ktc_ref[:, c * L:(c + 1) * L], vc_ref[c * L:(c + 1) * L, :])

    def lat_step(j, accumulate):
        for d in range(2):
            c = j if d == 0 else ncl - 1 - j
            off = pl.multiple_of(c * L, L)
            q_c = q_ref[pl.ds(off, L), :]
            kt_c = kt_ref[:, pl.ds(off, L)]
            v_c = v_ref[pl.ds(off, L), :]
            h = chunk_out(d, c, q_c, kt_c, v_c)
            if accumulate:
                o_ref[pl.ds(off, L), :] += h
            else:
                o_ref[pl.ds(off, L), :] = h
            update_state(d, c, kt_c, v_c)

    def first_half(j, carry):
        lat_step(j, False)
        return carry

    def second_half(j, carry):
        lat_step(j, True)
        return carry

    lax.fori_loop(0, ncl // 2, first_half, 0, unroll=2)
    lax.fori_loop(ncl // 2, ncl, second_half, 0, unroll=2)


def _mlstm(q, kt, v, ktc, vc, gl, gc, *, batch, t_lat, t_ctx):
    ncl = t_lat // CHUNK
    ncc = t_ctx // CHUNK
    nc = ncc + ncl
    assert ncl % 2 == 0
    dh = M_HEAD_DIM
    kern = functools.partial(_mlstm_kernel, ncc=ncc, ncl=ncl)
    return pl.pallas_call(
        kern,
        grid=(batch, M_HEADS),
        in_specs=[
            pl.BlockSpec((t_lat, dh), lambda b, h: (b, h)),
            pl.BlockSpec((dh, t_lat), lambda b, h: (h, b)),
            pl.BlockSpec((t_lat, dh), lambda b, h: (b, h)),
            pl.BlockSpec((dh, t_ctx), lambda b, h: (h, b)),
            pl.BlockSpec((t_ctx, dh), lambda b, h: (b, h)),
            pl.BlockSpec((4, None, None, ncl, CHUNK), lambda b, h: (0, h, b, 0, 0)),
            pl.BlockSpec((4, None, None, ncc, CHUNK), lambda b, h: (0, h, b, 0, 0)),
        ],
        out_specs=pl.BlockSpec((t_lat, dh), lambda b, h: (b, h)),
        out_shape=jax.ShapeDtypeStruct((batch * t_lat, M_WIDTH), F32),
        scratch_shapes=[
            pltpu.VMEM((2, dh, dh), F32),
            pltpu.VMEM((2, dh, LANES), F32),
            pltpu.VMEM((2, nc, CHUNK), F32),
            pltpu.VMEM((2, nc, CHUNK), F32),
            pltpu.VMEM((2, nc, CHUNK), F32),
            pltpu.VMEM((2, nc, LANES), F32),
            pltpu.VMEM((2, nc, LANES), F32),
        ],
        compiler_params=_params("parallel", "arbitrary"),
        name="mlstm",
    )(q, kt, v, ktc, vc, gl, gc)


ROW_BLOCK = 4
ROW_SUB = 8
GROUPS_PER_STEP = 2


def _fourier_kernel(x_ref, bd_ref, w2_ref, kc_ref, ks_ref, o_ref, zr_ref, zi_ref, *, rows, scale):
    gd = F_GROUP_DIM
    nb = ROW_BLOCK * GRID_W
    for blk in range(rows // ROW_BLOCK):
        pq = _dot(bd_ref[...], x_ref[blk * nb:(blk + 1) * nb, :])
        ar = pq[0:nb, :].astype(BF16)
        ai = pq[nb:, :].astype(BF16)
        rs = slice(blk * ROW_BLOCK, (blk + 1) * ROW_BLOCK)
        for g in range(GROUPS_PER_STEP):
            cs = slice(g * gd, (g + 1) * gd)
            z = _dot(jnp.concatenate([ar[:, cs], ai[:, cs]], axis=1), w2_ref[...])
            zr_ref[rs, :, cs] = z[:, 0:gd].reshape(ROW_BLOCK, GRID_W, gd)
            zi_ref[rs, :, cs] = z[:, gd:].reshape(ROW_BLOCK, GRID_W, gd)
    nw = GROUPS_PER_STEP * gd
    for j in range(GRID_W // ROW_SUB):
        js = slice(j * ROW_SUB, (j + 1) * ROW_SUB)
        zr = zr_ref[:, js, :].reshape(rows * ROW_SUB, nw).astype(BF16)
        zi = zi_ref[:, js, :].reshape(rows * ROW_SUB, nw).astype(BF16)
        y = _dot(kc_ref[...], zr) + _dot(ks_ref[...], zi)
        o_ref[:, js, :] = (y * scale).reshape(rows, ROW_SUB, nw)


def _dft_mats(n):
    k = np.arange(n)
    ang = 2.0 * np.pi * np.outer(k, k) / n
    return np.cos(ang), np.sin(ang)


def _fourier(xf, *, batch, t_lat):
    rows = t_lat // GRID_W
    assert rows % ROW_BLOCK == 0
    ck, sk = _dft_mats(F_GROUP_DIM)
    cc, sc = _dft_mats(GRID_W)
    cr, sr = _dft_mats(rows)
    eye_b = np.eye(ROW_BLOCK)
    eye_s = np.eye(ROW_SUB)
    tables = (
        np.concatenate([np.kron(eye_b, cc), np.kron(eye_b, -sc)], axis=0),
        np.block([[ck, -sk], [sk, ck]]),
        np.kron(cr, eye_s), np.kron(sr, eye_s),
    )
    bd, w2, kc, ks = (jnp.asarray(t, F32).astype(BF16) for t in tables)
    scale = 1.0 / math.sqrt(rows * GRID_W * F_GROUP_DIM)
    kern = functools.partial(_fourier_kernel, rows=rows, scale=scale)
    nw = GROUPS_PER_STEP * F_GROUP_DIM
    out = pl.pallas_call(
        kern,
        grid=(batch, F_GROUPS // GROUPS_PER_STEP),
        in_specs=[
            pl.BlockSpec((t_lat, nw), lambda b, g: (b, g)),
            _const_spec(bd.shape), _const_spec(w2.shape), _const_spec(kc.shape), _const_spec(ks.shape),
        ],
        out_specs=pl.BlockSpec((None, rows, GRID_W, nw), lambda b, g: (b, 0, 0, g)),
        out_shape=jax.ShapeDtypeStruct((batch, rows, GRID_W, F_WIDTH), F32),
        scratch_shapes=[
            pltpu.VMEM((rows, GRID_W, nw), F32),
            pltpu.VMEM((rows, GRID_W, nw), F32),
        ],
        compiler_params=_params("parallel", "parallel"),
        name="fourier",
    )(xf, bd, w2, kc, ks)
    return out.reshape(batch * t_lat, F_WIDTH)


def _merge_kernel(x_ref, yf_ref, h_ref, o_ref, gf_ref, gm_ref, mod_ref, g_ref, hg_ref,
                  wf_ref, wm_ref, wo_ref, out_ref, hm_ref):
    dh = M_HEAD_DIM
    for hd in range(M_HEADS):
        sl = slice(hd * dh, (hd + 1) * dh)
        hn = _rms(h_ref[:, sl], hg_ref[:, sl])
        hm_ref[:, sl] = (jax.nn.sigmoid(o_ref[:, sl].astype(F32)) * hn).astype(BF16)
    y = (jax.nn.sigmoid(gf_ref[...].astype(F32)) * _dot(yf_ref[...].astype(BF16), wf_ref[...])
         + jax.nn.sigmoid(gm_ref[...].astype(F32)) * _dot(hm_ref[...], wm_ref[...]))
    out = _dot(y.astype(BF16), wo_ref[...])
    out_ref[...] = x_ref[...] + mod_ref[5:6, :] * _rms(out, g_ref[3:4, :])


def _merge(x, yf, h, o, gf, gm, mod, g, hg, wf, wm, wo, *, tm, tiles_per_seq):
    n, d = x.shape
    tok = lambda width: pl.BlockSpec((tm, width), lambda i: (i, 0))
    return pl.pallas_call(
        _merge_kernel,
        grid=(n // tm,),
        in_specs=[
            tok(d), tok(F_WIDTH), tok(M_WIDTH), tok(M_WIDTH), tok(d), tok(d),
            pl.BlockSpec((None, N_ADA, d), lambda i: (i // tiles_per_seq, 0, 0)),
            _const_spec(g.shape), _const_spec(hg.shape),
            _const_spec(wf.shape), _const_spec(wm.shape), _const_spec(wo.shape),
        ],
        out_specs=tok(d),
        out_shape=jax.ShapeDtypeStruct((n, d), F32),
        scratch_shapes=[pltpu.VMEM((tm, M_WIDTH), BF16)],
        compiler_params=_params("parallel"),
        name="merge",
    )(x, yf, h, o, gf, gm, mod, g, hg, wf, wm, wo)


def _tile(t, cap):
    tm = min(t, cap)
    assert t % tm == 0
    return tm


def kernel(x, c, ctx, c_ctx, w_ada, b_ada, norm_g, w13_a, w2_a, w_in, b_in, conv_w, conv_b,
           head_g, w_four, w_mproj, w_out, w13_b, w2_b):
    batch, t_lat, d = x.shape
    t_ctx = ctx.shape[1]
    assert d == D_MODEL and w_ada.shape[0] == 1, "single-layer kernel"
    assert batch + 1 <= MOD_ROWS and t_lat % (GRID_W * CHUNK // math.gcd(GRID_W, CHUNK)) == 0
    assert t_ctx % CHUNK == 0

    cvec = jnp.concatenate([c, c_ctx[None], jnp.zeros((MOD_ROWS - batch - 1, d), F32)], axis=0)
    mod = _ada(cvec, w_ada[0], b_ada[0][None])
    g = norm_g[0]

    gpad = LANES - N_GATES
    wi, bi = w_in[0], b_in[0][None]
    w_p = (wi[:, :COL_GATES].astype(BF16),
           jnp.pad(wi[:, COL_GATES:COL_BR], ((0, 0), (0, gpad))).astype(BF16),
           wi[:, COL_BR:].astype(BF16))
    b_p = (bi[:, :COL_GATES], jnp.pad(bi[:, COL_GATES:COL_BR], ((0, 0), (0, gpad))), bi[:, COL_BR:])
    w13a, w2a = w13_a[0].astype(BF16), w2_a[0].astype(BF16)
    w13b, w2b = w13_b[0].astype(BF16), w2_b[0].astype(BF16)
    wf, wm, wo = w_four[0].astype(BF16), w_mproj[0].astype(BF16), w_out[0].astype(BF16)
    cw, cb, hg = conv_w[0], conv_b[0][None], head_g[0][None]

    tm_l = _tile(t_lat, 512)
    tm_c = _tile(t_ctx, 512)
    tps_l = t_lat // tm_l
    tps_c = t_ctx // tm_c
    xl = x.reshape(batch * t_lat, d)
    xc = ctx.reshape(batch * t_ctx, d)

    lat = dict(tm=tm_l, tiles_per_seq=tps_l, mod_row0=0, mod_step=1)
    con = dict(tm=tm_c, tiles_per_seq=tps_c, mod_row0=batch, mod_step=0)

    hl = _ffn(xl, mod, g, w13a, w2a, mod0=0, g0=0, name="ffn_a_lat", **lat)
    hc = _ffn(xc, mod, g, w13a, w2a, mod0=0, g0=0, name="ffn_a_ctx", **con)

    xf, q, kt, v, o, gt, gf, gm = _proj(hl, mod, g, w_p, b_p, cw, cb, full=True, name="proj_lat", **lat)
    ktc, vc, gtc = _proj(hc, mod, g, w_p, b_p, cw, cb, full=False, name="proj_ctx", **con)

    gl = gt.reshape(4, M_HEADS, batch, t_lat // CHUNK, CHUNK)
    gc = gtc.reshape(4, M_HEADS, batch, t_ctx // CHUNK, CHUNK)
    hm = _mlstm(q, kt, v, ktc, vc, gl, gc, batch=batch, t_lat=t_lat, t_ctx=t_ctx)
    yf = _fourier(xf, batch=batch, t_lat=t_lat)

    hl = _merge(hl, yf, hm, o, gf, gm, mod, g, hg, wf, wm, wo, tm=tm_l, tiles_per_seq=tps_l)
    hl = _ffn(hl, mod, g, w13b, w2b, mod0=6, g0=4, name="ffn_b_lat", **lat)
    return hl.reshape(batch, t_lat, d)
```

```python
import functools
import math

import numpy as np
import jax
import jax.numpy as jnp
from jax import lax
from jax.experimental import pallas as pl
from jax.experimental.pallas import tpu as pltpu

D_MODEL = 1024
GRID_W = 64
FF_HALF = 2816
F_GROUPS = 4
F_GROUP_DIM = 128
F_WIDTH = F_GROUPS * F_GROUP_DIM
M_HEADS = 4
M_HEAD_DIM = 256
M_WIDTH = M_HEADS * M_HEAD_DIM
CONV_K = 3
CHUNK = 128
N_ADA = 9
EPS = 1e-6
LOG2E = math.log2(math.e)

COL_Q = F_WIDTH
COL_K = COL_Q + M_WIDTH
COL_V = COL_K + M_WIDTH
COL_O = COL_V + M_WIDTH
COL_GATES = COL_O + M_WIDTH
N_GATES = 4 * M_HEADS
COL_BR = COL_GATES + N_GATES

LANES = 128
BF16_SUBLANES = 16
MOD_ROWS = 8
HALO = BF16_SUBLANES
VMEM_LIMIT = 56 * 1024 * 1024

F32 = jnp.float32
BF16 = jnp.bfloat16


def _const_spec(shape):
    nd = len(shape)
    return pl.BlockSpec(shape, lambda *_: (0,) * nd, pipeline_mode=pl.Buffered(1))


def _params(*sem):
    return pltpu.CompilerParams(dimension_semantics=sem, vmem_limit_bytes=VMEM_LIMIT)


def _rms(x, g):
    return x * lax.rsqrt(jnp.mean(x * x, axis=-1, keepdims=True) + EPS) * g


def _dot(a, b):
    return jnp.dot(a, b, preferred_element_type=F32)


def _ada_kernel(c_ref, w_ref, b_ref, o_ref):
    c = c_ref[...]
    s = (c * jax.nn.sigmoid(c)).astype(BF16)
    o_ref[...] = _dot(s, w_ref[...].astype(BF16)) + b_ref[...]


def _ada(cvec, w, b):
    d = cvec.shape[1]
    out = pl.pallas_call(
        _ada_kernel,
        grid=(N_ADA,),
        in_specs=[
            pl.BlockSpec((MOD_ROWS, d), lambda j: (0, 0)),
            pl.BlockSpec((d, d), lambda j: (0, j)),
            pl.BlockSpec((1, d), lambda j: (0, j)),
        ],
        out_specs=pl.BlockSpec((MOD_ROWS, d), lambda j: (0, j)),
        out_shape=jax.ShapeDtypeStruct((MOD_ROWS, N_ADA * d), F32),
        compiler_params=_params("arbitrary"),
        name="ada",
    )(cvec, w, b)
    return out.reshape(MOD_ROWS, N_ADA, d)


FF_CHUNK = 256
FFN_SUB = 512


def _ffn_kernel(x_ref, mod_ref, g_ref, w13_ref, w2_ref, o_ref, hid_ref, *, mod0, g0):
    shift = mod_ref[mod0:mod0 + 1, :]
    scale = mod_ref[mod0 + 1:mod0 + 2, :]
    gate = mod_ref[mod0 + 2:mod0 + 3, :]
    nsub, sub, _ = hid_ref.shape
    for s in range(nsub):
        rows = slice(s * sub, (s + 1) * sub)
        x = x_ref[rows, :]
        u = (_rms(x, g_ref[g0:g0 + 1, :]) * (1.0 + scale) + shift).astype(BF16)
        for c in range(FF_HALF // FF_CHUNK):
            lo = c * FF_CHUNK
            a = _dot(u, w13_ref[:, lo:lo + FF_CHUNK])
            b = _dot(u, w13_ref[:, FF_HALF + lo:FF_HALF + lo + FF_CHUNK])
            hid_ref[s, :, lo:lo + FF_CHUNK] = (a * jax.nn.sigmoid(a) * b).astype(BF16)
        y = _dot(hid_ref[s], w2_ref[...])
        o_ref[rows, :] = x + 0.5 * gate * _rms(y, g_ref[g0 + 1:g0 + 2, :])


def _ffn(h, mod, g, w13, w2, *, tm, tiles_per_seq, mod_row0, mod_step, mod0, g0, name):
    n, d = h.shape
    kern = functools.partial(_ffn_kernel, mod0=mod0, g0=g0)
    sub = min(tm, FFN_SUB)
    return pl.pallas_call(
        kern,
        grid=(n // tm,),
        in_specs=[
            pl.BlockSpec((tm, d), lambda i: (i, 0)),
            pl.BlockSpec((None, N_ADA, d), lambda i: (mod_row0 + mod_step * (i // tiles_per_seq), 0, 0)),
            _const_spec(g.shape),
            _const_spec(w13.shape),
            _const_spec(w2.shape),
        ],
        out_specs=pl.BlockSpec((tm, d), lambda i: (i, 0)),
        out_shape=jax.ShapeDtypeStruct((n, d), F32),
        scratch_shapes=[pltpu.VMEM((tm // sub, sub, FF_HALF), BF16)],
        compiler_params=_params("parallel"),
        name=name,
    )(h, mod, g, w13, w2)


QK_CHUNK = 256
SPLIT_ROWS = 128


def _split_w_in_kernel(w_ref, main_ref, gates_ref, br_ref):
    main_ref[...] = w_ref[:, 0:COL_GATES].astype(BF16)
    lane = lax.broadcasted_iota(jnp.int32, gates_ref.shape, 1)
    gates_ref[...] = jnp.where(lane < N_GATES, w_ref[:, COL_GATES:COL_GATES + LANES], 0.0).astype(BF16)
    br_ref[...] = w_ref[:, COL_BR:].astype(BF16)


def _split_w_in(w):
    d, width = w.shape
    widths = (COL_GATES, LANES, width - COL_BR)
    return pl.pallas_call(
        _split_w_in_kernel,
        grid=(d // SPLIT_ROWS,),
        in_specs=[pl.BlockSpec((SPLIT_ROWS, width), lambda i: (i, 0))],
        out_specs=[pl.BlockSpec((SPLIT_ROWS, n), lambda i: (i, 0)) for n in widths],
        out_shape=[jax.ShapeDtypeStruct((d, n), BF16) for n in widths],
        compiler_params=_params("parallel"),
        name="split_w_in",
    )(w)


def _proj_kernel(x_ref, xp_ref, xn_ref, mod_ref, g_ref, w_ref, wg_ref, wbr_ref, b_ref, bg_ref, bbr_ref,
                 cw_ref, cb_ref, *refs, tm, tiles_per_seq, full):
    if full:
        xf_ref, q_ref, kt_ref, v_ref, o_ref, gt_ref, gf_ref, gm_ref, u_ref = refs
    else:
        kt_ref, v_ref, gt_ref, u_ref = refs
    i = pl.program_id(0)
    first = (i % tiles_per_seq) == 0
    last = (i % tiles_per_seq) == tiles_per_seq - 1
    shift = mod_ref[3:4, :]
    scale = mod_ref[4:5, :]
    g = g_ref[2:3, :]

    def mk_u(x):
        return (_rms(x, g) * (1.0 + scale) + shift).astype(BF16)

    u_ref[0:HALO, :] = mk_u(xp_ref[...])
    u_ref[HALO:HALO + tm, :] = mk_u(x_ref[...])
    u_ref[HALO + tm:, :] = mk_u(xn_ref[...])
    u = u_ref[HALO:HALO + tm, :]

    def lin(lo, width):
        return _dot(u, w_ref[:, lo:lo + width]) + b_ref[:, lo:lo + width]

    v_ref[...] = lin(COL_V, M_WIDTH).astype(BF16)
    gt_ref[...] = (_dot(u, wg_ref[...]) + bg_ref[...]).T[0:N_GATES, :]
    if full:
        xf_ref[...] = lin(0, F_WIDTH).astype(BF16)
        o_ref[...] = lin(COL_O, M_WIDTH).astype(BF16)
        gf_ref[...] = (_dot(u, wbr_ref[:, 0:D_MODEL]) + bbr_ref[:, 0:D_MODEL]).astype(BF16)
        gm_ref[...] = (_dot(u, wbr_ref[:, D_MODEL:]) + bbr_ref[:, D_MODEL:]).astype(BF16)

    row = lax.broadcasted_iota(jnp.int32, (tm + 2 * HALO, 1), 0)
    lo_valid = jnp.where(first, HALO, 0)
    hi_valid = jnp.where(last, HALO + tm, tm + 2 * HALO)
    valid = jnp.logical_and(row >= lo_valid, row < hi_valid)
    u_ext = u_ref[...]
    for c in range(0 if full else M_WIDTH // QK_CHUNK, 2 * M_WIDTH // QK_CHUNK):
        lo = c * QK_CHUNK
        p = _dot(u_ext, w_ref[:, COL_Q + lo:COL_Q + lo + QK_CHUNK]) + b_ref[:, COL_Q + lo:COL_Q + lo + QK_CHUNK]
        p = jnp.where(valid, p, 0.0)
        cw = cw_ref[:, lo:lo + QK_CHUNK]
        y = (cw[0:1, :] * pltpu.roll(p, 1, 0) + cw[1:2, :] * p
             + cw[2:3, :] * pltpu.roll(p, tm + 2 * HALO - 1, 0))
        y = y[HALO:HALO + tm, :] + cb_ref[:, lo:lo + QK_CHUNK]
        y = y * jax.nn.sigmoid(y)
        if lo < M_WIDTH:
            q_ref[:, lo:lo + QK_CHUNK] = y.astype(BF16)
        else:
            kt_ref[lo - M_WIDTH:lo - M_WIDTH + QK_CHUNK, :] = (y * (M_HEAD_DIM ** -0.5)).T.astype(BF16)


def _proj(h, mod, g, ws, bs, cw, cb, *, tm, tiles_per_seq, mod_row0, mod_step, full, name):
    n, d = h.shape
    hb = tm // HALO
    nhb = n // HALO
    kern = functools.partial(_proj_kernel, tm=tm, tiles_per_seq=tiles_per_seq, full=full)
    tok = lambda width, dt: (pl.BlockSpec((tm, width), lambda i: (i, 0)), jax.ShapeDtypeStruct((n, width), dt))
    tr = lambda width, dt: (pl.BlockSpec((width, tm), lambda i: (0, i)), jax.ShapeDtypeStruct((width, n), dt))
    xf, q, o, gf, gm = tok(F_WIDTH, BF16), tok(M_WIDTH, BF16), tok(M_WIDTH, BF16), tok(d, BF16), tok(d, BF16)
    kt = tr(M_WIDTH, BF16)
    v = tok(M_WIDTH, BF16)
    gt = tr(N_GATES, F32)
    outs = (xf, q, kt, v, o, gt, gf, gm) if full else (kt, v, gt)
    return pl.pallas_call(
        kern,
        grid=(n // tm,),
        in_specs=[
            pl.BlockSpec((tm, d), lambda i: (i, 0)),
            pl.BlockSpec((HALO, d), lambda i: (jnp.maximum(i * hb - 1, 0), 0)),
            pl.BlockSpec((HALO, d), lambda i: (jnp.minimum((i + 1) * hb, nhb - 1), 0)),
            pl.BlockSpec((None, N_ADA, d), lambda i: (mod_row0 + mod_step * (i // tiles_per_seq), 0, 0)),
            _const_spec(g.shape),
            *[_const_spec(a.shape) for a in (*ws, *bs, cw, cb)],
        ],
        out_specs=[s for s, _ in outs],
        out_shape=[s for _, s in outs],
        scratch_shapes=[pltpu.VMEM((tm + 2 * HALO, d), BF16)],
        compiler_params=_params("parallel"),
        name=name,
    )(h, h, h, mod, g, *ws, *bs, cw, cb)


def _lane_scan(x, reverse):
    lane = lax.broadcasted_iota(jnp.int32, x.shape, 1)
    k = 1
    while k < CHUNK:
        if reverse:
            x = x + jnp.where(lane < CHUNK - k, pltpu.roll(x, CHUNK - k, 1), 0.0)
        else:
            x = x + jnp.where(lane >= k, pltpu.roll(x, k, 1), 0.0)
        k *= 2
    return x


def _mlstm_kernel(q_ref, kt_ref, v_ref, ktc_ref, vc_ref, gl_ref, gc_ref, o_ref,
                  c_ref, n_ref, r_ref, wk_ref, lf_ref, ms_ref, dec_ref,
                  num_ref, den_ref, a_ref, e_ref, *, ncc, ncl):
    L = CHUNK
    dv = M_HEAD_DIM
    half = ncl // 2
    t_idx = lax.broadcasted_iota(jnp.int32, (L, L), 0)
    s_idx = lax.broadcasted_iota(jnp.int32, (L, L), 1)
    masks = (s_idx <= t_idx, s_idx >= t_idx)

    def gate_rows(g_ref, d):
        li = g_ref[2 * d] * LOG2E
        gf = g_ref[2 * d + 1]
        lf = (jnp.minimum(gf, 0.0) - jnp.log1p(jnp.exp(-jnp.abs(gf)))) * LOG2E
        bcum = _lane_scan(lf, reverse=(d == 1))
        r = li - bcum
        rmax = jnp.max(r, axis=-1, keepdims=True)
        b_end = bcum[:, L - 1:L] if d == 0 else bcum[:, 0:1]
        return lf, r, rmax, b_end

    for d in range(2):
        parts = (gate_rows(gl_ref, d), gate_rows(gc_ref, d))
        rev = d == 1
        order = [(1, c) for c in (range(ncc - 1, -1, -1) if rev else range(ncc))]
        order += [(0, c) for c in (range(ncl - 1, -1, -1) if rev else range(ncl))]
        m = jnp.zeros((1, 1), F32)
        for part, c in order:
            row = c + (ncl if part == 1 else 0)
            ms_ref[d, row:row + 1, :] = jnp.broadcast_to(m, (1, LANES))
            _, _, rmax, b_end = parts[part]
            m = b_end[c:c + 1, :] + jnp.maximum(m, rmax[c:c + 1, :])
        for part, (lo, cnt) in enumerate(((0, ncl), (ncl, ncc))):
            lf, r, rmax, _ = parts[part]
            m_start = ms_ref[d, lo:lo + cnt, 0:1]
            m_end = jnp.maximum(m_start, rmax)
            dec_ref[d, lo:lo + cnt, :] = jnp.broadcast_to(jnp.exp2(m_start - m_end), (cnt, LANES))
            wk_ref[d, lo:lo + cnt, :] = jnp.exp2(r - m_end)
            r_ref[d, lo:lo + cnt, :] = r
            lf_ref[d, lo:lo + cnt, :] = lf

    c_ref[...] = jnp.zeros_like(c_ref)
    n_ref[...] = jnp.zeros_like(n_ref)

    def update_state(d, row, kt_c, v_c):
        wk = wk_ref[d, pl.ds(row, 1), :]
        dec = dec_ref[d, pl.ds(row, 1), :]
        ktw = kt_c.astype(F32) * wk
        c_ref[d] = jnp.concatenate([dec] * (dv // LANES), axis=1) * c_ref[d] + _dot(ktw.astype(BF16), v_c)
        n_ref[d] = dec * n_ref[d] + jnp.sum(ktw, axis=-1, keepdims=True)

    def chunk_refs(c):
        off = pl.multiple_of(c * L, L)
        return off, q_ref[pl.ds(off, L), :], kt_ref[:, pl.ds(off, L)], v_ref[pl.ds(off, L), :]

    def intra(d, c, slot):
        _, q_c, kt_c, v_c = chunk_refs(c)
        rows = pl.ds(pl.multiple_of(slot * L, L), L)
        r_row = r_ref[d, pl.ds(c, 1), :]
        lf_row = lf_ref[d, pl.ds(c, 1), :]
        m = ms_ref[d, pl.ds(c, 1), :]
        rm = jnp.where(masks[d], r_row, -jnp.inf)
        mt = jnp.maximum(m, jnp.max(rm, axis=-1, keepdims=True))
        bc = jnp.sum(jnp.where(masks[d], lf_row, 0.0), axis=-1, keepdims=True)
        s = _dot(q_c, kt_c) * jnp.exp2(rm - mt)
        num_ref[d, rows, :] = _dot(s.astype(BF16), v_c)
        den_ref[d, rows, :] = jnp.broadcast_to(jnp.sum(s, axis=-1, keepdims=True), (L, LANES))
        a_ref[d, rows, :] = jnp.exp2(m - mt)
        e_ref[d, rows, :] = jnp.exp2(-bc - mt)

    def inter(d, c, slot, accumulate):
        off, q_c, kt_c, v_c = chunk_refs(c)
        rows = pl.ds(pl.multiple_of(slot * L, L), L)
        a = a_ref[d, rows, :]
        state = jnp.concatenate([c_ref[d].astype(BF16), n_ref[d].astype(BF16)], axis=1)
        qcn = _dot(q_c, state)
        den = a * qcn[:, dv:] + den_ref[d, rows, :]
        inv = 1.0 / jnp.maximum(jnp.abs(den), e_ref[d, rows, :])
        for k in range(0, dv, LANES):
            h = (a * qcn[:, k:k + LANES] + num_ref[d, rows, k:k + LANES]) * inv
            if accumulate:
                o_ref[pl.ds(off, L), k:k + LANES] += h
            else:
                o_ref[pl.ds(off, L), k:k + LANES] = h
        update_state(d, c, kt_c, v_c)

    for d in range(2):
        for c in (range(ncc) if d == 0 else range(ncc - 1, -1, -1)):
            update_state(d, ncl + c, ktc_ref[:, c * L:(c + 1) * L], vc_ref[c * L:(c + 1) * L, :])

    def run_half(first, accumulate):
        def chunk_of(d, g):
            return first + g if d == 0 else ncl - 1 - first - g

        def intra_step(g, carry):
            for d in range(2):
                intra(d, chunk_of(d, g), g)
            return carry

        def inter_step(g, carry):
            for d in range(2):
                inter(d, chunk_of(d, g), g, accumulate)
            return carry

        lax.fori_loop(0, half, intra_step, 0, unroll=4)
        lax.fori_loop(0, half, inter_step, 0, unroll=4)

    run_half(0, False)
    run_half(half, True)


def _mlstm(q, kt, v, ktc, vc, gl, gc, *, batch, t_lat, t_ctx):
    ncl = t_lat // CHUNK
    ncc = t_ctx // CHUNK
    nc = ncc + ncl
    assert ncl % 4 == 0
    half_rows = t_lat // 2
    dh = M_HEAD_DIM
    kern = functools.partial(_mlstm_kernel, ncc=ncc, ncl=ncl)
    return pl.pallas_call(
        kern,
        grid=(batch, M_HEADS),
        in_specs=[
            pl.BlockSpec((t_lat, dh), lambda b, h: (b, h)),
            pl.BlockSpec((dh, t_lat), lambda b, h: (h, b)),
            pl.BlockSpec((t_lat, dh), lambda b, h: (b, h)),
            pl.BlockSpec((dh, t_ctx), lambda b, h: (h, b)),
            pl.BlockSpec((t_ctx, dh), lambda b, h: (b, h)),
            pl.BlockSpec((4, None, None, ncl, CHUNK), lambda b, h: (0, h, b, 0, 0)),
            pl.BlockSpec((4, None, None, ncc, CHUNK), lambda b, h: (0, h, b, 0, 0)),
        ],
        out_specs=pl.BlockSpec((t_lat, dh), lambda b, h: (b, h)),
        out_shape=jax.ShapeDtypeStruct((batch * t_lat, M_WIDTH), F32),
        scratch_shapes=[
            pltpu.VMEM((2, dh, dh), F32),
            pltpu.VMEM((2, dh, LANES), F32),
            pltpu.VMEM((2, nc, CHUNK), F32),
            pltpu.VMEM((2, nc, CHUNK), F32),
            pltpu.VMEM((2, nc, CHUNK), F32),
            pltpu.VMEM((2, nc, LANES), F32),
            pltpu.VMEM((2, nc, LANES), F32),
            pltpu.VMEM((2, half_rows, dh), F32),
            pltpu.VMEM((2, half_rows, LANES), F32),
            pltpu.VMEM((2, half_rows, LANES), F32),
            pltpu.VMEM((2, half_rows, LANES), F32),
        ],
        compiler_params=_params("parallel", "arbitrary"),
        name="mlstm",
    )(q, kt, v, ktc, vc, gl, gc)


ROW_BLOCK = 4
ROW_SUB = 8
GROUPS_PER_STEP = 2


def _fourier_kernel(x_ref, bd_ref, w2_ref, kc_ref, ks_ref, o_ref, zr_ref, zi_ref, *, rows, scale):
    gd = F_GROUP_DIM
    nb = ROW_BLOCK * GRID_W
    for blk in range(rows // ROW_BLOCK):
        pq = _dot(bd_ref[...], x_ref[blk * nb:(blk + 1) * nb, :])
        ar = pq[0:nb, :].astype(BF16)
        ai = pq[nb:, :].astype(BF16)
        rs = slice(blk * ROW_BLOCK, (blk + 1) * ROW_BLOCK)
        for g in range(GROUPS_PER_STEP):
            cs = slice(g * gd, (g + 1) * gd)
            z = _dot(jnp.concatenate([ar[:, cs], ai[:, cs]], axis=1), w2_ref[...])
            zr_ref[rs, :, cs] = z[:, 0:gd].reshape(ROW_BLOCK, GRID_W, gd)
            zi_ref[rs, :, cs] = z[:, gd:].reshape(ROW_BLOCK, GRID_W, gd)
    nw = GROUPS_PER_STEP * gd
    for j in range(GRID_W // ROW_SUB):
        js = slice(j * ROW_SUB, (j + 1) * ROW_SUB)
        zr = zr_ref[:, js, :].reshape(rows * ROW_SUB, nw).astype(BF16)
        zi = zi_ref[:, js, :].reshape(rows * ROW_SUB, nw).astype(BF16)
        y = _dot(kc_ref[...], zr) + _dot(ks_ref[...], zi)
        o_ref[:, js, :] = (y * scale).reshape(rows, ROW_SUB, nw)


def _dft_mats(n):
    k = np.arange(n)
    ang = 2.0 * np.pi * np.outer(k, k) / n
    return np.cos(ang), np.sin(ang)


def _fourier(xf, *, batch, t_lat):
    rows = t_lat // GRID_W
    assert rows % ROW_BLOCK == 0
    ck, sk = _dft_mats(F_GROUP_DIM)
    cc, sc = _dft_mats(GRID_W)
    cr, sr = _dft_mats(rows)
    eye_b = np.eye(ROW_BLOCK)
    eye_s = np.eye(ROW_SUB)
    tables = (
        np.concatenate([np.kron(eye_b, cc), np.kron(eye_b, -sc)], axis=0),
        np.block([[ck, -sk], [sk, ck]]),
        np.kron(cr, eye_s), np.kron(sr, eye_s),
    )
    bd, w2, kc, ks = (jnp.asarray(t, F32).astype(BF16) for t in tables)
    scale = 1.0 / math.sqrt(rows * GRID_W * F_GROUP_DIM)
    kern = functools.partial(_fourier_kernel, rows=rows, scale=scale)
    nw = GROUPS_PER_STEP * F_GROUP_DIM
    out = pl.pallas_call(
        kern,
        grid=(batch, F_GROUPS // GROUPS_PER_STEP),
        in_specs=[
            pl.BlockSpec((t_lat, nw), lambda b, g: (b, g)),
            _const_spec(bd.shape), _const_spec(w2.shape), _const_spec(kc.shape), _const_spec(ks.shape),
        ],
        out_specs=pl.BlockSpec((None, rows, GRID_W, nw), lambda b, g: (b, 0, 0, g)),
        out_shape=jax.ShapeDtypeStruct((batch, rows, GRID_W, F_WIDTH), F32),
        scratch_shapes=[
            pltpu.VMEM((rows, GRID_W, nw), F32),
            pltpu.VMEM((rows, GRID_W, nw), F32),
        ],
        compiler_params=_params("parallel", "parallel"),
        name="fourier",
    )(xf, bd, w2, kc, ks)
    return out.reshape(batch * t_lat, F_WIDTH)


def _merge_kernel(x_ref, yf_ref, h_ref, o_ref, gf_ref, gm_ref, mod_ref, g_ref, hg_ref,
                  wf_ref, wm_ref, wo_ref, out_ref, hm_ref):
    dh = M_HEAD_DIM
    for hd in range(M_HEADS):
        sl = slice(hd * dh, (hd + 1) * dh)
        hn = _rms(h_ref[:, sl], hg_ref[:, sl])
        hm_ref[:, sl] = (jax.nn.sigmoid(o_ref[:, sl].astype(F32)) * hn).astype(BF16)
    y = (jax.nn.sigmoid(gf_ref[...].astype(F32)) * _dot(yf_ref[...].astype(BF16), wf_ref[...])
         + jax.nn.sigmoid(gm_ref[...].astype(F32)) * _dot(hm_ref[...], wm_ref[...]))
    out = _dot(y.astype(BF16), wo_ref[...])
    out_ref[...] = x_ref[...] + mod_ref[5:6, :] * _rms(out, g_ref[3:4, :])


def _merge(x, yf, h, o, gf, gm, mod, g, hg, wf, wm, wo, *, tm, tiles_per_seq):
    n, d = x.shape
    tok = lambda width: pl.BlockSpec((tm, width), lambda i: (i, 0))
    return pl.pallas_call(
        _merge_kernel,
        grid=(n // tm,),
        in_specs=[
            tok(d), tok(F_WIDTH), tok(M_WIDTH), tok(M_WIDTH), tok(d), tok(d),
            pl.BlockSpec((None, N_ADA, d), lambda i: (i // tiles_per_seq, 0, 0)),
            _const_spec(g.shape), _const_spec(hg.shape),
            _const_spec(wf.shape), _const_spec(wm.shape), _const_spec(wo.shape),
        ],
        out_specs=tok(d),
        out_shape=jax.ShapeDtypeStruct((n, d), F32),
        scratch_shapes=[pltpu.VMEM((tm, M_WIDTH), BF16)],
        compiler_params=_params("parallel"),
        name="merge",
    )(x, yf, h, o, gf, gm, mod, g, hg, wf, wm, wo)


def _tile(t, cap):
    tm = min(t, cap)
    assert t % tm == 0
    return tm


def kernel(x, c, ctx, c_ctx, w_ada, b_ada, norm_g, w13_a, w2_a, w_in, b_in, conv_w, conv_b,
           head_g, w_four, w_mproj, w_out, w13_b, w2_b):
    batch, t_lat, d = x.shape
    t_ctx = ctx.shape[1]
    assert d == D_MODEL and w_ada.shape[0] == 1, "single-layer kernel"
    assert batch + 1 <= MOD_ROWS and t_lat % (GRID_W * CHUNK // math.gcd(GRID_W, CHUNK)) == 0
    assert t_ctx % CHUNK == 0

    cvec = jnp.concatenate([c, c_ctx[None], jnp.zeros((MOD_ROWS - batch - 1, d), F32)], axis=0)
    mod = _ada(cvec, w_ada[0], b_ada[0][None])
    g = norm_g[0]

    gpad = LANES - N_GATES
    bi = b_in[0][None]
    w_p = _split_w_in(w_in[0])
    b_p = (bi[:, :COL_GATES], jnp.pad(bi[:, COL_GATES:COL_BR], ((0, 0), (0, gpad))), bi[:, COL_BR:])
    w13a, w2a = w13_a[0].astype(BF16), w2_a[0].astype(BF16)
    w13b, w2b = w13_b[0].astype(BF16), w2_b[0].astype(BF16)
    wf, wm, wo = w_four[0].astype(BF16), w_mproj[0].astype(BF16), w_out[0].astype(BF16)
    cw, cb, hg = conv_w[0], conv_b[0][None], head_g[0][None]

    tm_l = _tile(t_lat, 512)
    tm_c = _tile(t_ctx, 512)
    tps_l = t_lat // tm_l
    tps_c = t_ctx // tm_c
    xl = x.reshape(batch * t_lat, d)
    xc = ctx.reshape(batch * t_ctx, d)

    lat = dict(tm=tm_l, tiles_per_seq=tps_l, mod_row0=0, mod_step=1)
    con = dict(tm=tm_c, tiles_per_seq=tps_c, mod_row0=batch, mod_step=0)
    tm_f = _tile(t_lat, 2 * FFN_SUB)
    lat_ffn = dict(lat, tm=tm_f, tiles_per_seq=t_lat // tm_f)

    hl = _ffn(xl, mod, g, w13a, w2a, mod0=0, g0=0, name="ffn_a_lat", **lat_ffn)
    hc = _ffn(xc, mod, g, w13a, w2a, mod0=0, g0=0, name="ffn_a_ctx", **con)

    xf, q, kt, v, o, gt, gf, gm = _proj(hl, mod, g, w_p, b_p, cw, cb, full=True, name="proj_lat", **lat)
    ktc, vc, gtc = _proj(hc, mod, g, w_p, b_p, cw, cb, full=False, name="proj_ctx", **con)

    gl = gt.reshape(4, M_HEADS, batch, t_lat // CHUNK, CHUNK)
    gc = gtc.reshape(4, M_HEADS, batch, t_ctx // CHUNK, CHUNK)
    hm = _mlstm(q, kt, v, ktc, vc, gl, gc, batch=batch, t_lat=t_lat, t_ctx=t_ctx)
    yf = _fourier(xf, batch=batch, t_lat=t_lat)

    hl = _merge(hl, yf, hm, o, gf, gm, mod, g, hg, wf, wm, wo, tm=tm_l, tiles_per_seq=tps_l)
    hl = _ffn(hl, mod, g, w13b, w2b, mod0=6, g0=4, name="ffn_b_lat", **lat_ffn)
    return hl.reshape(batch, t_lat, d)
```

```python
import functools
import math

import numpy as np
import jax
import jax.numpy as jnp
from jax import lax
from jax.experimental import pallas as pl
from jax.experimental.pallas import tpu as pltpu

D_MODEL = 1024
GRID_W = 64
FF_HALF = 2816
F_GROUPS = 4
F_GROUP_DIM = 128
F_WIDTH = F_GROUPS * F_GROUP_DIM
M_HEADS = 4
M_HEAD_DIM = 256
M_WIDTH = M_HEADS * M_HEAD_DIM
CONV_K = 3
CHUNK = 128
N_ADA = 9
EPS = 1e-6
LOG2E = math.log2(math.e)

COL_Q = F_WIDTH
COL_K = COL_Q + M_WIDTH
COL_V = COL_K + M_WIDTH
COL_O = COL_V + M_WIDTH
COL_GATES = COL_O + M_WIDTH
N_GATES = 4 * M_HEADS
COL_BR = COL_GATES + N_GATES

LANES = 128
F32_SUBLANES = 8
BF16_SUBLANES = 16
MOD_ROWS = 8
HALO = BF16_SUBLANES
VMEM_LIMIT = 56 * 1024 * 1024

F32 = jnp.float32
BF16 = jnp.bfloat16


def _const_spec(shape):
    nd = len(shape)
    return pl.BlockSpec(shape, lambda *_: (0,) * nd, pipeline_mode=pl.Buffered(1))


def _params(*sem):
    return pltpu.CompilerParams(dimension_semantics=sem, vmem_limit_bytes=VMEM_LIMIT)


def _rms(x, g):
    return x * lax.rsqrt(jnp.mean(x * x, axis=-1, keepdims=True) + EPS) * g


def _dot(a, b):
    return jnp.dot(a, b, preferred_element_type=F32)


def _ada_kernel(c_ref, w_ref, b_ref, o_ref):
    c = c_ref[...]
    s = (c * jax.nn.sigmoid(c)).astype(BF16)
    o_ref[...] = _dot(s, w_ref[...].astype(BF16)) + b_ref[...]


def _ada(cvec, w, b):
    d = cvec.shape[1]
    out = pl.pallas_call(
        _ada_kernel,
        grid=(N_ADA,),
        in_specs=[
            pl.BlockSpec((MOD_ROWS, d), lambda j: (0, 0)),
            pl.BlockSpec((d, d), lambda j: (0, j)),
            pl.BlockSpec((1, d), lambda j: (0, j)),
        ],
        out_specs=pl.BlockSpec((MOD_ROWS, d), lambda j: (0, j)),
        out_shape=jax.ShapeDtypeStruct((MOD_ROWS, N_ADA * d), F32),
        compiler_params=_params("arbitrary"),
        name="ada",
    )(cvec, w, b)
    return out.reshape(MOD_ROWS, N_ADA, d)


FF_CHUNK = 256
FFN_SUB = 512


def _ffn_kernel(x_ref, mod_ref, g_ref, w13_ref, w2_ref, o_ref, hid_ref, *, mod0, g0):
    shift = mod_ref[mod0:mod0 + 1, :]
    scale = mod_ref[mod0 + 1:mod0 + 2, :]
    gate = mod_ref[mod0 + 2:mod0 + 3, :]
    nsub, sub, _ = hid_ref.shape
    rows = lambda s: slice(s * sub, (s + 1) * sub)

    for s in range(nsub):
        x = x_ref[rows(s), :]
        u = (_rms(x, g_ref[g0:g0 + 1, :]) * (1.0 + scale) + shift).astype(BF16)
        for c in range(FF_HALF // FF_CHUNK):
            lo = c * FF_CHUNK
            a = _dot(u, w13_ref[:, lo:lo + FF_CHUNK])
            b = _dot(u, w13_ref[:, FF_HALF + lo:FF_HALF + lo + FF_CHUNK])
            hid_ref[s, :, lo:lo + FF_CHUNK] = (a * jax.nn.sigmoid(a) * b).astype(BF16)
        y = _dot(hid_ref[s], w2_ref[...])
        o_ref[rows(s), :] = x + 0.5 * gate * _rms(y, g_ref[g0 + 1:g0 + 2, :])


def _ffn(h, mod, g, w13, w2, *, tm, tiles_per_seq, mod_row0, mod_step, mod0, g0, name):
    n, d = h.shape
    kern = functools.partial(_ffn_kernel, mod0=mod0, g0=g0)
    sub = min(tm, FFN_SUB)
    return pl.pallas_call(
        kern,
        grid=(n // tm,),
        in_specs=[
            pl.BlockSpec((tm, d), lambda i: (i, 0)),
            pl.BlockSpec((None, N_ADA, d), lambda i: (mod_row0 + mod_step * (i // tiles_per_seq), 0, 0)),
            _const_spec(g.shape),
            _const_spec(w13.shape),
            _const_spec(w2.shape),
        ],
        out_specs=pl.BlockSpec((tm, d), lambda i: (i, 0)),
        out_shape=jax.ShapeDtypeStruct((n, d), F32),
        scratch_shapes=[pltpu.VMEM((tm // sub, sub, FF_HALF), BF16)],
        compiler_params=_params("parallel"),
        name=name,
    )(h, mod, g, w13, w2)


QK_CHUNK = 256
SPLIT_COLS = 512


def _split_w_in_kernel(main_t_ref, gates_t_ref, br_t_ref, main_ref, gates_ref, br_ref):
    main_ref[...] = main_t_ref[...].T.astype(BF16)
    gates_ref[...] = gates_t_ref[...].T.astype(BF16)
    br_ref[...] = br_t_ref[...].T.astype(BF16)


def _split_w_in(wt):
    width, d = wt.shape
    n_br = width - COL_BR
    gates_t = jnp.pad(wt[COL_GATES:COL_BR], ((0, LANES - N_GATES), (0, 0)))
    br_t = wt[COL_BR:]
    steps = COL_GATES // SPLIT_COLS
    last_br = n_br // SPLIT_COLS - 1
    return pl.pallas_call(
        _split_w_in_kernel,
        grid=(steps,),
        in_specs=[pl.BlockSpec((SPLIT_COLS, d), lambda i: (i, 0)),
                  pl.BlockSpec((LANES, d), lambda i: (0, 0)),
                  pl.BlockSpec((SPLIT_COLS, d), lambda i: (jnp.minimum(i, last_br), 0))],
        out_specs=[pl.BlockSpec((d, SPLIT_COLS), lambda i: (0, i)),
                   pl.BlockSpec((d, LANES), lambda i: (0, 0)),
                   pl.BlockSpec((d, SPLIT_COLS), lambda i: (0, jnp.minimum(i, last_br)))],
        out_shape=[jax.ShapeDtypeStruct((d, n), BF16) for n in (COL_GATES, LANES, n_br)],
        compiler_params=_params("arbitrary"),
        name="split_w_in",
    )(wt, gates_t, br_t)


def _proj_kernel(x_ref, xp_ref, xn_ref, mod_ref, g_ref, w_ref, wg_ref, wbr_ref, b_ref, bg_ref, bbr_ref,
                 cw_ref, cb_ref, *refs, tm, tiles_per_seq, full):
    if full:
        xf_ref, q_ref, kt_ref, v_ref, o_ref, gt_ref, gf_ref, gm_ref, u_ref, p_ref = refs
    else:
        kt_ref, v_ref, gt_ref, u_ref, p_ref = refs
    i = pl.program_id(0)
    first = (i % tiles_per_seq) == 0
    last = (i % tiles_per_seq) == tiles_per_seq - 1
    shift = mod_ref[3:4, :]
    scale = mod_ref[4:5, :]
    g = g_ref[2:3, :]

    def mk_u(x):
        return (_rms(x, g) * (1.0 + scale) + shift).astype(BF16)

    u_ref[0:HALO, :] = mk_u(xp_ref[...])
    u_ref[HALO:HALO + tm, :] = mk_u(x_ref[...])
    u_ref[HALO + tm:, :] = mk_u(xn_ref[...])
    u = u_ref[HALO:HALO + tm, :]

    def lin(dst_ref, src_ref, bias_ref, lo, k):
        cols = slice(lo + k, lo + k + QK_CHUNK)
        dst_ref[:, k:k + QK_CHUNK] = (_dot(u, src_ref[:, cols]) + bias_ref[:, cols]).astype(dst_ref.dtype)

    def gates():
        gt_ref[...] = (_dot(u, wg_ref[...]) + bg_ref[...]).T[0:N_GATES, :]

    top = slice(HALO - F32_SUBLANES, HALO)
    bot = slice(HALO + tm, HALO + tm + F32_SUBLANES)

    def conv(c):
        lo = c * QK_CHUNK
        pb = p_ref.at[c % 2]
        p = _dot(u_ref[...], w_ref[:, COL_Q + lo:COL_Q + lo + QK_CHUNK]) + b_ref[:, COL_Q + lo:COL_Q + lo + QK_CHUNK]
        pb[...] = p
        pb[top, :] = jnp.where(first, 0.0, p[top, :])
        pb[bot, :] = jnp.where(last, 0.0, p[bot, :])
        cw = cw_ref[:, lo:lo + QK_CHUNK]
        y = (cw[0:1, :] * pb[HALO - 1:HALO - 1 + tm, :] + cw[1:2, :] * pb[HALO:HALO + tm, :]
             + cw[2:3, :] * pb[HALO + 1:HALO + 1 + tm, :]) + cb_ref[:, lo:lo + QK_CHUNK]
        y = y * jax.nn.sigmoid(y)
        if lo < M_WIDTH:
            q_ref[:, lo:lo + QK_CHUNK] = y.astype(BF16)
        else:
            kt_ref[lo - M_WIDTH:lo - M_WIDTH + QK_CHUNK, :] = (y * (M_HEAD_DIM ** -0.5)).T.astype(BF16)

    chunks = lambda width: range(0, width, QK_CHUNK)
    plain = [functools.partial(lin, v_ref, w_ref, b_ref, COL_V, k) for k in chunks(M_WIDTH)] + [gates]
    if full:
        plain += [functools.partial(lin, xf_ref, w_ref, b_ref, 0, k) for k in chunks(F_WIDTH)]
        plain += [functools.partial(lin, o_ref, w_ref, b_ref, COL_O, k) for k in chunks(M_WIDTH)]
        plain += [functools.partial(lin, gf_ref, wbr_ref, bbr_ref, 0, k) for k in chunks(D_MODEL)]
        plain += [functools.partial(lin, gm_ref, wbr_ref, bbr_ref, D_MODEL, k) for k in chunks(D_MODEL)]
    convs = list(range(0 if full else M_WIDTH // QK_CHUNK, 2 * M_WIDTH // QK_CHUNK))
    per_conv = -(-len(plain) // len(convs))
    for n_done, c in enumerate(convs):
        conv(c)
        for task in plain[n_done * per_conv:(n_done + 1) * per_conv]:
            task()


def _proj(h, mod, g, ws, bs, cw, cb, *, tm, tiles_per_seq, mod_row0, mod_step, full, name):
    n, d = h.shape
    hb = tm // HALO
    nhb = n // HALO
    kern = functools.partial(_proj_kernel, tm=tm, tiles_per_seq=tiles_per_seq, full=full)
    tok = lambda width, dt: (pl.BlockSpec((tm, width), lambda i: (i, 0)), jax.ShapeDtypeStruct((n, width), dt))
    tr = lambda width, dt: (pl.BlockSpec((width, tm), lambda i: (0, i)), jax.ShapeDtypeStruct((width, n), dt))
    xf, q, o, gf, gm = tok(F_WIDTH, BF16), tok(M_WIDTH, BF16), tok(M_WIDTH, BF16), tok(d, BF16), tok(d, BF16)
    kt = tr(M_WIDTH, BF16)
    v = tok(M_WIDTH, BF16)
    gt = tr(N_GATES, F32)
    outs = (xf, q, kt, v, o, gt, gf, gm) if full else (kt, v, gt)
    return pl.pallas_call(
        kern,
        grid=(n // tm,),
        in_specs=[
            pl.BlockSpec((tm, d), lambda i: (i, 0)),
            pl.BlockSpec((HALO, d), lambda i: (jnp.maximum(i * hb - 1, 0), 0)),
            pl.BlockSpec((HALO, d), lambda i: (jnp.minimum((i + 1) * hb, nhb - 1), 0)),
            pl.BlockSpec((None, N_ADA, d), lambda i: (mod_row0 + mod_step * (i // tiles_per_seq), 0, 0)),
            _const_spec(g.shape),
            *[_const_spec(a.shape) for a in (*ws, *bs, cw, cb)],
        ],
        out_specs=[s for s, _ in outs],
        out_shape=[s for _, s in outs],
        scratch_shapes=[pltpu.VMEM((tm + 2 * HALO, d), BF16),
                        pltpu.VMEM((2, tm + 2 * HALO, QK_CHUNK), F32)],
        compiler_params=_params("parallel"),
        name=name,
    )(h, h, h, mod, g, *ws, *bs, cw, cb)


def _lane_scan(x, reverse):
    lane = lax.broadcasted_iota(jnp.int32, x.shape, 1)
    k = 1
    while k < CHUNK:
        if reverse:
            x = x + jnp.where(lane < CHUNK - k, pltpu.roll(x, CHUNK - k, 1), 0.0)
        else:
            x = x + jnp.where(lane >= k, pltpu.roll(x, k, 1), 0.0)
        k *= 2
    return x


def _mlstm_kernel(q_ref, kt_ref, v_ref, ktc_ref, vc_ref, gl_ref, gc_ref, o_ref,
                  c_ref, n_ref, r_ref, wk_ref, lf_ref, ms_ref, dec_ref,
                  num_ref, den_ref, a_ref, e_ref, *, ncc, ncl):
    L = CHUNK
    dv = M_HEAD_DIM
    half = ncl // 2
    t_idx = lax.broadcasted_iota(jnp.int32, (L, L), 0)
    s_idx = lax.broadcasted_iota(jnp.int32, (L, L), 1)
    masks = (s_idx <= t_idx, s_idx >= t_idx)

    def gate_rows(g_ref, d):
        li = g_ref[2 * d] * LOG2E
        gf = g_ref[2 * d + 1]
        lf = (jnp.minimum(gf, 0.0) - jnp.log1p(jnp.exp(-jnp.abs(gf)))) * LOG2E
        bcum = _lane_scan(lf, reverse=(d == 1))
        r = li - bcum
        rmax = jnp.max(r, axis=-1, keepdims=True)
        b_end = bcum[:, L - 1:L] if d == 0 else bcum[:, 0:1]
        return lf, r, rmax, b_end

    for d in range(2):
        parts = (gate_rows(gl_ref, d), gate_rows(gc_ref, d))
        rev = d == 1
        order = [(1, c) for c in (range(ncc - 1, -1, -1) if rev else range(ncc))]
        order += [(0, c) for c in (range(ncl - 1, -1, -1) if rev else range(ncl))]
        m = jnp.zeros((1, 1), F32)
        for part, c in order:
            row = c + (ncl if part == 1 else 0)
            ms_ref[d, row:row + 1, :] = jnp.broadcast_to(m, (1, LANES))
            _, _, rmax, b_end = parts[part]
            m = b_end[c:c + 1, :] + jnp.maximum(m, rmax[c:c + 1, :])
        for part, (lo, cnt) in enumerate(((0, ncl), (ncl, ncc))):
            lf, r, rmax, _ = parts[part]
            m_start = ms_ref[d, lo:lo + cnt, 0:1]
            m_end = jnp.maximum(m_start, rmax)
            dec_ref[d, lo:lo + cnt, :] = jnp.broadcast_to(jnp.exp2(m_start - m_end), (cnt, LANES))
            wk_ref[d, lo:lo + cnt, :] = jnp.exp2(r - m_end)
            r_ref[d, lo:lo + cnt, :] = r
            lf_ref[d, lo:lo + cnt, :] = lf

    c_ref[...] = jnp.zeros_like(c_ref)
    n_ref[...] = jnp.zeros_like(n_ref)

    def update_state(d, row, kt_c, v_c):
        wk = wk_ref[d, pl.ds(row, 1), :]
        dec = dec_ref[d, pl.ds(row, 1), :]
        ktw = kt_c.astype(F32) * wk
        c_ref[d] = jnp.concatenate([dec] * (dv // LANES), axis=1) * c_ref[d] + _dot(ktw.astype(BF16), v_c)
        n_ref[d] = dec * n_ref[d] + jnp.sum(ktw, axis=-1, keepdims=True)

    def chunk_refs(c):
        off = pl.multiple_of(c * L, L)
        return off, q_ref[pl.ds(off, L), :], kt_ref[:, pl.ds(off, L)], v_ref[pl.ds(off, L), :]

    def intra(d, c, slot):
        _, q_c, kt_c, v_c = chunk_refs(c)
        rows = pl.ds(pl.multiple_of(slot * L, L), L)
        r_row = r_ref[d, pl.ds(c, 1), :]
        lf_row = lf_ref[d, pl.ds(c, 1), :]
        m = ms_ref[d, pl.ds(c, 1), :]
        rm = jnp.where(masks[d], r_row, -jnp.inf)
        mt = jnp.maximum(m, jnp.max(rm, axis=-1, keepdims=True))
        bc = jnp.sum(jnp.where(masks[d], lf_row, 0.0), axis=-1, keepdims=True)
        s = _dot(q_c, kt_c) * jnp.exp2(rm - mt)
        num_ref[d, rows, :] = _dot(s.astype(BF16), v_c)
        den_ref[d, rows, :] = jnp.broadcast_to(jnp.sum(s, axis=-1, keepdims=True), (L, LANES))
        a_ref[d, rows, :] = jnp.exp2(m - mt)
        e_ref[d, rows, :] = jnp.exp2(-bc - mt)

    def inter(d, c, slot, accumulate):
        off, q_c, kt_c, v_c = chunk_refs(c)
        rows = pl.ds(pl.multiple_of(slot * L, L), L)
        a = a_ref[d, rows, :]
        state = jnp.concatenate([c_ref[d].astype(BF16), n_ref[d].astype(BF16)], axis=1)
        qcn = _dot(q_c, state)
        den = a * qcn[:, dv:] + den_ref[d, rows, :]
        inv = 1.0 / jnp.maximum(jnp.abs(den), e_ref[d, rows, :])
        for k in range(0, dv, LANES):
            h = (a * qcn[:, k:k + LANES] + num_ref[d, rows, k:k + LANES]) * inv
            if accumulate:
                o_ref[pl.ds(off, L), k:k + LANES] += h
            else:
                o_ref[pl.ds(off, L), k:k + LANES] = h
        update_state(d, c, kt_c, v_c)

    for d in range(2):
        for c in (range(ncc) if d == 0 else range(ncc - 1, -1, -1)):
            update_state(d, ncl + c, ktc_ref[:, c * L:(c + 1) * L], vc_ref[c * L:(c + 1) * L, :])

    def run_half(first, accumulate):
        def chunk_of(d, g):
            return first + g if d == 0 else ncl - 1 - first - g

        def intra_step(g, carry):
            for d in range(2):
                intra(d, chunk_of(d, g), g)
            return carry

        def inter_step(g, carry):
            for d in range(2):
                inter(d, chunk_of(d, g), g, accumulate)
            return carry

        lax.fori_loop(0, half, intra_step, 0, unroll=4)
        lax.fori_loop(0, half, inter_step, 0, unroll=4)

    run_half(0, False)
    run_half(half, True)


def _mlstm(q, kt, v, ktc, vc, gl, gc, *, batch, t_lat, t_ctx):
    ncl = t_lat // CHUNK
    ncc = t_ctx // CHUNK
    nc = ncc + ncl
    assert ncl % 4 == 0
    half_rows = t_lat // 2
    dh = M_HEAD_DIM
    kern = functools.partial(_mlstm_kernel, ncc=ncc, ncl=ncl)
    return pl.pallas_call(
        kern,
        grid=(batch, M_HEADS),
        in_specs=[
            pl.BlockSpec((t_lat, dh), lambda b, h: (b, h)),
            pl.BlockSpec((dh, t_lat), lambda b, h: (h, b)),
            pl.BlockSpec((t_lat, dh), lambda b, h: (b, h)),
            pl.BlockSpec((dh, t_ctx), lambda b, h: (h, b)),
            pl.BlockSpec((t_ctx, dh), lambda b, h: (b, h)),
            pl.BlockSpec((4, None, None, ncl, CHUNK), lambda b, h: (0, h, b, 0, 0)),
            pl.BlockSpec((4, None, None, ncc, CHUNK), lambda b, h: (0, h, b, 0, 0)),
        ],
        out_specs=pl.BlockSpec((t_lat, dh), lambda b, h: (b, h)),
        out_shape=jax.ShapeDtypeStruct((batch * t_lat, M_WIDTH), F32),
        scratch_shapes=[
            pltpu.VMEM((2, dh, dh), F32),
            pltpu.VMEM((2, dh, LANES), F32),
            pltpu.VMEM((2, nc, CHUNK), F32),
            pltpu.VMEM((2, nc, CHUNK), F32),
            pltpu.VMEM((2, nc, CHUNK), F32),
            pltpu.VMEM((2, nc, LANES), F32),
            pltpu.VMEM((2, nc, LANES), F32),
            pltpu.VMEM((2, half_rows, dh), F32),
            pltpu.VMEM((2, half_rows, LANES), F32),
            pltpu.VMEM((2, half_rows, LANES), F32),
            pltpu.VMEM((2, half_rows, LANES), F32),
        ],
        compiler_params=_params("parallel", "arbitrary"),
        name="mlstm",
    )(q, kt, v, ktc, vc, gl, gc)


ROW_BLOCK = 4
ROW_SUB = 8
GROUPS_PER_STEP = 2


def _fourier_kernel(x_ref, bd_ref, w2_ref, kc_ref, ks_ref, o_ref, zr_ref, zi_ref, *, rows, scale):
    gd = F_GROUP_DIM
    nb = ROW_BLOCK * GRID_W
    for blk in range(rows // ROW_BLOCK):
        pq = _dot(bd_ref[...], x_ref[blk * nb:(blk + 1) * nb, :])
        ar = pq[0:nb, :].astype(BF16)
        ai = pq[nb:, :].astype(BF16)
        rs = slice(blk * ROW_BLOCK, (blk + 1) * ROW_BLOCK)
        for g in range(GROUPS_PER_STEP):
            cs = slice(g * gd, (g + 1) * gd)
            z = _dot(jnp.concatenate([ar[:, cs], ai[:, cs]], axis=1), w2_ref[...])
            zr_ref[rs, :, cs] = z[:, 0:gd].reshape(ROW_BLOCK, GRID_W, gd)
            zi_ref[rs, :, cs] = z[:, gd:].reshape(ROW_BLOCK, GRID_W, gd)
    nw = GROUPS_PER_STEP * gd
    for j in range(GRID_W // ROW_SUB):
        js = slice(j * ROW_SUB, (j + 1) * ROW_SUB)
        zr = zr_ref[:, js, :].reshape(rows * ROW_SUB, nw).astype(BF16)
        zi = zi_ref[:, js, :].reshape(rows * ROW_SUB, nw).astype(BF16)
        y = _dot(kc_ref[...], zr) + _dot(ks_ref[...], zi)
        o_ref[:, js, :] = (y * scale).reshape(rows, ROW_SUB, nw)


def _dft_mats(n):
    k = np.arange(n)
    ang = 2.0 * np.pi * np.outer(k, k) / n
    return np.cos(ang), np.sin(ang)


def _fourier(xf, *, batch, t_lat):
    rows = t_lat // GRID_W
    assert rows % ROW_BLOCK == 0
    ck, sk = _dft_mats(F_GROUP_DIM)
    cc, sc = _dft_mats(GRID_W)
    cr, sr = _dft_mats(rows)
    eye_b = np.eye(ROW_BLOCK)
    eye_s = np.eye(ROW_SUB)
    tables = (
        np.concatenate([np.kron(eye_b, cc), np.kron(eye_b, -sc)], axis=0),
        np.block([[ck, -sk], [sk, ck]]),
        np.kron(cr, eye_s), np.kron(sr, eye_s),
    )
    bd, w2, kc, ks = (jnp.asarray(t, F32).astype(BF16) for t in tables)
    scale = 1.0 / math.sqrt(rows * GRID_W * F_GROUP_DIM)
    kern = functools.partial(_fourier_kernel, rows=rows, scale=scale)
    nw = GROUPS_PER_STEP * F_GROUP_DIM
    out = pl.pallas_call(
        kern,
        grid=(batch, F_GROUPS // GROUPS_PER_STEP),
        in_specs=[
            pl.BlockSpec((t_lat, nw), lambda b, g: (b, g)),
            _const_spec(bd.shape), _const_spec(w2.shape), _const_spec(kc.shape), _const_spec(ks.shape),
        ],
        out_specs=pl.BlockSpec((None, rows, GRID_W, nw), lambda b, g: (b, 0, 0, g)),
        out_shape=jax.ShapeDtypeStruct((batch, rows, GRID_W, F_WIDTH), F32),
        scratch_shapes=[
            pltpu.VMEM((rows, GRID_W, nw), F32),
            pltpu.VMEM((rows, GRID_W, nw), F32),
        ],
        compiler_params=_params("parallel", "parallel"),
        name="fourier",
    )(xf, bd, w2, kc, ks)
    return out.reshape(batch * t_lat, F_WIDTH)


def _merge_kernel(x_ref, yf_ref, h_ref, o_ref, gf_ref, gm_ref, mod_ref, g_ref, hg_ref,
                  wf_ref, wm_ref, wo_ref, out_ref, hm_ref):
    dh = M_HEAD_DIM
    for hd in range(M_HEADS):
        sl = slice(hd * dh, (hd + 1) * dh)
        hn = _rms(h_ref[:, sl], hg_ref[:, sl])
        hm_ref[:, sl] = (jax.nn.sigmoid(o_ref[:, sl].astype(F32)) * hn).astype(BF16)
    y = (jax.nn.sigmoid(gf_ref[...].astype(F32)) * _dot(yf_ref[...].astype(BF16), wf_ref[...])
         + jax.nn.sigmoid(gm_ref[...].astype(F32)) * _dot(hm_ref[...], wm_ref[...]))
    out = _dot(y.astype(BF16), wo_ref[...])
    out_ref[...] = x_ref[...] + mod_ref[5:6, :] * _rms(out, g_ref[3:4, :])


def _merge(x, yf, h, o, gf, gm, mod, g, hg, wf, wm, wo, *, tm, tiles_per_seq):
    n, d = x.shape
    tok = lambda width: pl.BlockSpec((tm, width), lambda i: (i, 0))
    return pl.pallas_call(
        _merge_kernel,
        grid=(n // tm,),
        in_specs=[
            tok(d), tok(F_WIDTH), tok(M_WIDTH), tok(M_WIDTH), tok(d), tok(d),
            pl.BlockSpec((None, N_ADA, d), lambda i: (i // tiles_per_seq, 0, 0)),
            _const_spec(g.shape), _const_spec(hg.shape),
            _const_spec(wf.shape), _const_spec(wm.shape), _const_spec(wo.shape),
        ],
        out_specs=tok(d),
        out_shape=jax.ShapeDtypeStruct((n, d), F32),
        scratch_shapes=[pltpu.VMEM((tm, M_WIDTH), BF16)],
        compiler_params=_params("parallel"),
        name="merge",
    )(x, yf, h, o, gf, gm, mod, g, hg, wf, wm, wo)


def _tile(t, cap):
    tm = min(t, cap)
    assert t % tm == 0
    return tm


def kernel(x, c, ctx, c_ctx, w_ada, b_ada, norm_g, w13_a, w2_a, w_in, b_in, conv_w, conv_b,
           head_g, w_four, w_mproj, w_out, w13_b, w2_b):
    batch, t_lat, d = x.shape
    t_ctx = ctx.shape[1]
    assert d == D_MODEL and w_ada.shape[0] == 1, "single-layer kernel"
    assert batch + 1 <= MOD_ROWS and t_lat % (GRID_W * CHUNK // math.gcd(GRID_W, CHUNK)) == 0
    assert t_ctx % CHUNK == 0

    cvec = jnp.concatenate([c, c_ctx[None], jnp.zeros((MOD_ROWS - batch - 1, d), F32)], axis=0)
    mod = _ada(cvec, w_ada[0], b_ada[0][None])
    g = norm_g[0]

    gpad = LANES - N_GATES
    bi = b_in[0][None]
    w_p = _split_w_in(w_in[0].T)
    b_p = (bi[:, :COL_GATES], jnp.pad(bi[:, COL_GATES:COL_BR], ((0, 0), (0, gpad))), bi[:, COL_BR:])
    w13a, w2a = w13_a[0].astype(BF16), w2_a[0].astype(BF16)
    w13b, w2b = w13_b[0].astype(BF16), w2_b[0].astype(BF16)
    wf, wm, wo = w_four[0].astype(BF16), w_mproj[0].astype(BF16), w_out[0].astype(BF16)
    cw, cb, hg = conv_w[0], conv_b[0][None], head_g[0][None]

    tm_l = _tile(t_lat, 512)
    tm_c = _tile(t_ctx, 512)
    tps_l = t_lat // tm_l
    tps_c = t_ctx // tm_c
    xl = x.reshape(batch * t_lat, d)
    xc = ctx.reshape(batch * t_ctx, d)

    lat = dict(tm=tm_l, tiles_per_seq=tps_l, mod_row0=0, mod_step=1)
    con = dict(tm=tm_c, tiles_per_seq=tps_c, mod_row0=batch, mod_step=0)
    tm_f = _tile(t_lat, 2 * FFN_SUB)
    lat_ffn = dict(lat, tm=tm_f, tiles_per_seq=t_lat // tm_f)
    con_ffn = dict(con, tm=_tile(batch * t_ctx, 2 * FFN_SUB))

    hl = _ffn(xl, mod, g, w13a, w2a, mod0=0, g0=0, name="ffn_a_lat", **lat_ffn)
    hc = _ffn(xc, mod, g, w13a, w2a, mod0=0, g0=0, name="ffn_a_ctx", **con_ffn)

    xf, q, kt, v, o, gt, gf, gm = _proj(hl, mod, g, w_p, b_p, cw, cb, full=True, name="proj_lat", **lat)
    ktc, vc, gtc = _proj(hc, mod, g, w_p, b_p, cw, cb, full=False, name="proj_ctx", **con)

    gl = gt.reshape(4, M_HEADS, batch, t_lat // CHUNK, CHUNK)
    gc = gtc.reshape(4, M_HEADS, batch, t_ctx // CHUNK, CHUNK)
    hm = _mlstm(q, kt, v, ktc, vc, gl, gc, batch=batch, t_lat=t_lat, t_ctx=t_ctx)
    yf = _fourier(xf, batch=batch, t_lat=t_lat)

    hl = _merge(hl, yf, hm, o, gf, gm, mod, g, hg, wf, wm, wo, tm=tm_l, tiles_per_seq=tps_l)
    hl = _ffn(hl, mod, g, w13b, w2b, mod0=6, g0=4, name="ffn_b_lat", **lat_ffn)
    return hl.reshape(batch, t_lat, d)
```

```python
import functools
import math
from typing import Any, Callable, NamedTuple

import numpy as np
import jax
import jax.numpy as jnp
from jax import lax
from jax.experimental import pallas as pl
from jax.experimental.pallas import tpu as pltpu

D_MODEL = 1024
GRID_W = 64
FF_HALF = 2816
F_GROUPS = 4
F_GROUP_DIM = 128
F_WIDTH = F_GROUPS * F_GROUP_DIM
M_HEADS = 4
M_HEAD_DIM = 256
M_WIDTH = M_HEADS * M_HEAD_DIM
CONV_K = 3
CHUNK = 128
N_ADA = 9
EPS = 1e-6
LOG2E = math.log2(math.e)

COL_Q = F_WIDTH
COL_K = COL_Q + M_WIDTH
COL_V = COL_K + M_WIDTH
COL_O = COL_V + M_WIDTH
COL_GATES = COL_O + M_WIDTH
N_GATES = 4 * M_HEADS
COL_BR = COL_GATES + N_GATES

LANES = 128
F32_SUBLANES = 8
BF16_SUBLANES = 16
MOD_ROWS = 8
HALO = BF16_SUBLANES
VMEM_LIMIT = 56 * 1024 * 1024

F32 = jnp.float32
BF16 = jnp.bfloat16


def _const_spec(shape):
    nd = len(shape)
    return pl.BlockSpec(shape, lambda *_: (0,) * nd, pipeline_mode=pl.Buffered(1))


def _params(*sem):
    return pltpu.CompilerParams(dimension_semantics=sem, vmem_limit_bytes=VMEM_LIMIT)


def _rms(x, g):
    return x * lax.rsqrt(jnp.mean(x * x, axis=-1, keepdims=True) + EPS) * g


def _dot(a, b):
    return jnp.dot(a, b, preferred_element_type=F32)


class _SideCast(NamedTuple):
    src: Any
    in_block: tuple
    in_index: Callable[[Any], tuple]
    out_shape: tuple
    out_block: tuple
    out_index: Callable[[Any], tuple]
    transpose: bool


def _num_blocks(units, steps):
    return max(k for k in range(1, min(units, steps) + 1) if units % k == 0)


def _row_cast(w, steps):
    rows, cols = w.shape
    nb = _num_blocks(rows // BF16_SUBLANES, steps)
    rb = rows // nb
    idx = lambda s: (jnp.minimum(s, nb - 1), 0)
    return _SideCast(w, (rb, cols), idx, (rows, cols), (rb, cols), idx, False)


def _transposed_cast(wt, row0, n, steps):
    nb = _num_blocks(n // LANES, steps)
    cb = n // nb
    d = wt.shape[1]
    assert row0 % F32_SUBLANES == 0
    row = lambda s: pl.multiple_of(row0 + jnp.minimum(s, nb - 1) * cb, F32_SUBLANES)
    return _SideCast(wt, (pl.Element(cb), pl.Element(d)), lambda s: (row(s), 0),
                     (d, n), (d, cb), lambda s: (0, jnp.minimum(s, nb - 1)), True)


def _side_specs(jobs, step_of):
    ins = [pl.BlockSpec(j.in_block, lambda *g, j=j: j.in_index(step_of(*g))) for j in jobs]
    outs = [pl.BlockSpec(j.out_block, lambda *g, j=j: j.out_index(step_of(*g))) for j in jobs]
    shapes = [jax.ShapeDtypeStruct(j.out_shape, BF16) for j in jobs]
    return ins, outs, shapes


def _run_side_casts(in_refs, out_refs, transposes):
    for i_ref, o_ref, transpose in zip(in_refs, out_refs, transposes):
        v = i_ref[...]
        o_ref[...] = (v.T if transpose else v).astype(BF16)


def _ada_kernel(c_ref, w_ref, b_ref, o_ref):
    c = c_ref[...]
    s = (c * jax.nn.sigmoid(c)).astype(BF16)
    o_ref[...] = _dot(s, w_ref[...].astype(BF16)) + b_ref[...]


def _ada(cvec, w, b):
    d = cvec.shape[1]
    out = pl.pallas_call(
        _ada_kernel,
        grid=(N_ADA,),
        in_specs=[
            pl.BlockSpec((MOD_ROWS, d), lambda j: (0, 0)),
            pl.BlockSpec((d, d), lambda j: (0, j)),
            pl.BlockSpec((1, d), lambda j: (0, j)),
        ],
        out_specs=pl.BlockSpec((MOD_ROWS, d), lambda j: (0, j)),
        out_shape=jax.ShapeDtypeStruct((MOD_ROWS, N_ADA * d), F32),
        compiler_params=_params("arbitrary"),
        name="ada",
    )(cvec, w, b)
    return out.reshape(MOD_ROWS, N_ADA, d)


FF_CHUNK = 256
FFN_SUB = 512


def _ffn_kernel(x_ref, mod_ref, g_ref, w13_ref, w2_ref, *refs, mod0, g0, side):
    ns = len(side)
    o_ref, hid_ref = refs[ns], refs[2 * ns + 1]
    _run_side_casts(refs[:ns], refs[ns + 1:2 * ns + 1], side)
    shift = mod_ref[mod0:mod0 + 1, :]
    scale = mod_ref[mod0 + 1:mod0 + 2, :]
    gate = mod_ref[mod0 + 2:mod0 + 3, :]
    nsub, sub, _ = hid_ref.shape
    rows = lambda s: slice(s * sub, (s + 1) * sub)

    for s in range(nsub):
        x = x_ref[rows(s), :]
        u = (_rms(x, g_ref[g0:g0 + 1, :]) * (1.0 + scale) + shift).astype(BF16)
        for c in range(FF_HALF // FF_CHUNK):
            lo = c * FF_CHUNK
            a = _dot(u, w13_ref[:, lo:lo + FF_CHUNK])
            b = _dot(u, w13_ref[:, FF_HALF + lo:FF_HALF + lo + FF_CHUNK])
            hid_ref[s, :, lo:lo + FF_CHUNK] = (a * jax.nn.sigmoid(a) * b).astype(BF16)
        y = _dot(hid_ref[s], w2_ref[...])
        o_ref[rows(s), :] = x + 0.5 * gate * _rms(y, g_ref[g0 + 1:g0 + 2, :])


def _ffn(h, mod, g, w13, w2, *, tm, tiles_per_seq, mod_row0, mod_step, mod0, g0, name, side_jobs=None):
    n, d = h.shape
    jobs = side_jobs(n // tm) if side_jobs else ()
    s_in, s_out, s_shape = _side_specs(jobs, lambda i: i)
    kern = functools.partial(_ffn_kernel, mod0=mod0, g0=g0, side=tuple(j.transpose for j in jobs))
    sub = min(tm, FFN_SUB)
    outs = pl.pallas_call(
        kern,
        grid=(n // tm,),
        in_specs=[
            pl.BlockSpec((tm, d), lambda i: (i, 0)),
            pl.BlockSpec((None, N_ADA, d), lambda i: (mod_row0 + mod_step * (i // tiles_per_seq), 0, 0)),
            _const_spec(g.shape),
            _const_spec(w13.shape),
            _const_spec(w2.shape),
            *s_in,
        ],
        out_specs=[pl.BlockSpec((tm, d), lambda i: (i, 0)), *s_out],
        out_shape=[jax.ShapeDtypeStruct((n, d), F32), *s_shape],
        scratch_shapes=[pltpu.VMEM((tm // sub, sub, FF_HALF), BF16)],
        compiler_params=_params("arbitrary"),
        name=name,
    )(h, mod, g, w13, w2, *[j.src for j in jobs])
    return outs if jobs else outs[0]


QK_CHUNK = 256


def _w_in_jobs(wt, steps):
    return (_transposed_cast(wt, 0, COL_GATES, steps),
            _transposed_cast(wt, COL_GATES, LANES, steps),
            _transposed_cast(wt, COL_BR, wt.shape[0] - COL_BR, steps))


def _proj_kernel(x_ref, xp_ref, xn_ref, mod_ref, g_ref, w_ref, wg_ref, wbr_ref, b_ref, bg_ref, bbr_ref,
                 cw_ref, cb_ref, *refs, tm, tiles_per_seq, full):
    if full:
        xf_ref, q_ref, kt_ref, v_ref, o_ref, gt_ref, gf_ref, gm_ref, u_ref, p_ref = refs
    else:
        kt_ref, v_ref, gt_ref, u_ref, p_ref = refs
    i = pl.program_id(0)
    first = (i % tiles_per_seq) == 0
    last = (i % tiles_per_seq) == tiles_per_seq - 1
    shift = mod_ref[3:4, :]
    scale = mod_ref[4:5, :]
    g = g_ref[2:3, :]

    def mk_u(x):
        return (_rms(x, g) * (1.0 + scale) + shift).astype(BF16)

    u_ref[0:HALO, :] = mk_u(xp_ref[...])
    u_ref[HALO:HALO + tm, :] = mk_u(x_ref[...])
    u_ref[HALO + tm:, :] = mk_u(xn_ref[...])
    u = u_ref[HALO:HALO + tm, :]

    def lin(dst_ref, src_ref, bias_ref, lo, k):
        cols = slice(lo + k, lo + k + QK_CHUNK)
        dst_ref[:, k:k + QK_CHUNK] = (_dot(u, src_ref[:, cols]) + bias_ref[:, cols]).astype(dst_ref.dtype)

    def gates():
        gt_ref[...] = (_dot(u, wg_ref[...]) + bg_ref[...]).T[0:N_GATES, :]

    top = slice(HALO - F32_SUBLANES, HALO)
    bot = slice(HALO + tm, HALO + tm + F32_SUBLANES)

    def conv(c):
        lo = c * QK_CHUNK
        pb = p_ref.at[c % 2]
        p = _dot(u_ref[...], w_ref[:, COL_Q + lo:COL_Q + lo + QK_CHUNK]) + b_ref[:, COL_Q + lo:COL_Q + lo + QK_CHUNK]
        pb[...] = p
        pb[top, :] = jnp.where(first, 0.0, p[top, :])
        pb[bot, :] = jnp.where(last, 0.0, p[bot, :])
        cw = cw_ref[:, lo:lo + QK_CHUNK]
        y = (cw[0:1, :] * pb[HALO - 1:HALO - 1 + tm, :] + cw[1:2, :] * pb[HALO:HALO + tm, :]
             + cw[2:3, :] * pb[HALO + 1:HALO + 1 + tm, :]) + cb_ref[:, lo:lo + QK_CHUNK]
        y = y * jax.nn.sigmoid(y)
        if lo < M_WIDTH:
            q_ref[:, lo:lo + QK_CHUNK] = y.astype(BF16)
        else:
            kt_ref[lo - M_WIDTH:lo - M_WIDTH + QK_CHUNK, :] = (y * (M_HEAD_DIM ** -0.5)).T.astype(BF16)

    chunks = lambda width: range(0, width, QK_CHUNK)
    plain = [functools.partial(lin, v_ref, w_ref, b_ref, COL_V, k) for k in chunks(M_WIDTH)] + [gates]
    if full:
        plain += [functools.partial(lin, xf_ref, w_ref, b_ref, 0, k) for k in chunks(F_WIDTH)]
        plain += [functools.partial(lin, o_ref, w_ref, b_ref, COL_O, k) for k in chunks(M_WIDTH)]
        plain += [functools.partial(lin, gf_ref, wbr_ref, bbr_ref, 0, k) for k in chunks(D_MODEL)]
        plain += [functools.partial(lin, gm_ref, wbr_ref, bbr_ref, D_MODEL, k) for k in chunks(D_MODEL)]
    convs = list(range(0 if full else M_WIDTH // QK_CHUNK, 2 * M_WIDTH // QK_CHUNK))
    per_conv = -(-len(plain) // len(convs))
    for n_done, c in enumerate(convs):
        conv(c)
        for task in plain[n_done * per_conv:(n_done + 1) * per_conv]:
            task()


def _proj(h, mod, g, ws, bs, cw, cb, *, tm, tiles_per_seq, mod_row0, mod_step, full, name):
    n, d = h.shape
    hb = tm // HALO
    nhb = n // HALO
    kern = functools.partial(_proj_kernel, tm=tm, tiles_per_seq=tiles_per_seq, full=full)
    tok = lambda width, dt: (pl.BlockSpec((tm, width), lambda i: (i, 0)), jax.ShapeDtypeStruct((n, width), dt))
    tr = lambda width, dt: (pl.BlockSpec((width, tm), lambda i: (0, i)), jax.ShapeDtypeStruct((width, n), dt))
    xf, q, o, gf, gm = tok(F_WIDTH, BF16), tok(M_WIDTH, BF16), tok(M_WIDTH, BF16), tok(d, BF16), tok(d, BF16)
    kt = tr(M_WIDTH, BF16)
    v = tok(M_WIDTH, BF16)
    gt = tr(N_GATES, F32)
    outs = (xf, q, kt, v, o, gt, gf, gm) if full else (kt, v, gt)
    return pl.pallas_call(
        kern,
        grid=(n // tm,),
        in_specs=[
            pl.BlockSpec((tm, d), lambda i: (i, 0)),
            pl.BlockSpec((HALO, d), lambda i: (jnp.maximum(i * hb - 1, 0), 0)),
            pl.BlockSpec((HALO, d), lambda i: (jnp.minimum((i + 1) * hb, nhb - 1), 0)),
            pl.BlockSpec((None, N_ADA, d), lambda i: (mod_row0 + mod_step * (i // tiles_per_seq), 0, 0)),
            _const_spec(g.shape),
            *[_const_spec(a.shape) for a in (*ws, *bs, cw, cb)],
        ],
        out_specs=[s for s, _ in outs],
        out_shape=[s for _, s in outs],
        scratch_shapes=[pltpu.VMEM((tm + 2 * HALO, d), BF16),
                        pltpu.VMEM((2, tm + 2 * HALO, QK_CHUNK), F32)],
        compiler_params=_params("parallel"),
        name=name,
    )(h, h, h, mod, g, *ws, *bs, cw, cb)


def _lane_scan(x, reverse):
    lane = lax.broadcasted_iota(jnp.int32, x.shape, 1)
    k = 1
    while k < CHUNK:
        if reverse:
            x = x + jnp.where(lane < CHUNK - k, pltpu.roll(x, CHUNK - k, 1), 0.0)
        else:
            x = x + jnp.where(lane >= k, pltpu.roll(x, k, 1), 0.0)
        k *= 2
    return x


def _mlstm_kernel(q_ref, kt_ref, v_ref, ktc_ref, vc_ref, gl_ref, gc_ref, o_ref,
                  c_ref, n_ref, r_ref, wk_ref, lf_ref, ms_ref, dec_ref,
                  num_ref, den_ref, a_ref, e_ref, *, ncc, ncl):
    L = CHUNK
    dv = M_HEAD_DIM
    half = ncl // 2
    t_idx = lax.broadcasted_iota(jnp.int32, (L, L), 0)
    s_idx = lax.broadcasted_iota(jnp.int32, (L, L), 1)
    masks = (s_idx <= t_idx, s_idx >= t_idx)

    def gate_rows(g_ref, d):
        li = g_ref[2 * d] * LOG2E
        gf = g_ref[2 * d + 1]
        lf = (jnp.minimum(gf, 0.0) - jnp.log1p(jnp.exp(-jnp.abs(gf)))) * LOG2E
        bcum = _lane_scan(lf, reverse=(d == 1))
        r = li - bcum
        rmax = jnp.max(r, axis=-1, keepdims=True)
        b_end = bcum[:, L - 1:L] if d == 0 else bcum[:, 0:1]
        return lf, r, rmax, b_end

    for d in range(2):
        parts = (gate_rows(gl_ref, d), gate_rows(gc_ref, d))
        rev = d == 1
        order = [(1, c) for c in (range(ncc - 1, -1, -1) if rev else range(ncc))]
        order += [(0, c) for c in (range(ncl - 1, -1, -1) if rev else range(ncl))]
        m = jnp.zeros((1, 1), F32)
        for part, c in order:
            row = c + (ncl if part == 1 else 0)
            ms_ref[d, row:row + 1, :] = jnp.broadcast_to(m, (1, LANES))
            _, _, rmax, b_end = parts[part]
            m = b_end[c:c + 1, :] + jnp.maximum(m, rmax[c:c + 1, :])
        for part, (lo, cnt) in enumerate(((0, ncl), (ncl, ncc))):
            lf, r, rmax, _ = parts[part]
            m_start = ms_ref[d, lo:lo + cnt, 0:1]
            m_end = jnp.maximum(m_start, rmax)
            dec_ref[d, lo:lo + cnt, :] = jnp.broadcast_to(jnp.exp2(m_start - m_end), (cnt, LANES))
            wk_ref[d, lo:lo + cnt, :] = jnp.exp2(r - m_end)
            r_ref[d, lo:lo + cnt, :] = r
            lf_ref[d, lo:lo + cnt, :] = lf

    c_ref[...] = jnp.zeros_like(c_ref)
    n_ref[...] = jnp.zeros_like(n_ref)

    def update_state(d, row, kt_c, v_c):
        wk = wk_ref[d, pl.ds(row, 1), :]
        dec = dec_ref[d, pl.ds(row, 1), :]
        ktw = kt_c.astype(F32) * wk
        c_ref[d] = jnp.concatenate([dec] * (dv // LANES), axis=1) * c_ref[d] + _dot(ktw.astype(BF16), v_c)
        n_ref[d] = dec * n_ref[d] + jnp.sum(ktw, axis=-1, keepdims=True)

    def chunk_refs(c):
        off = pl.multiple_of(c * L, L)
        return off, q_ref[pl.ds(off, L), :], kt_ref[:, pl.ds(off, L)], v_ref[pl.ds(off, L), :]

    def intra(d, c, slot):
        _, q_c, kt_c, v_c = chunk_refs(c)
        rows = pl.ds(pl.multiple_of(slot * L, L), L)
        r_row = r_ref[d, pl.ds(c, 1), :]
        lf_row = lf_ref[d, pl.ds(c, 1), :]
        m = ms_ref[d, pl.ds(c, 1), :]
        rm = jnp.where(masks[d], r_row, -jnp.inf)
        mt = jnp.maximum(m, jnp.max(rm, axis=-1, keepdims=True))
        bc = jnp.sum(jnp.where(masks[d], lf_row, 0.0), axis=-1, keepdims=True)
        s = _dot(q_c, kt_c) * jnp.exp2(rm - mt)
        num_ref[d, rows, :] = _dot(s.astype(BF16), v_c)
        den_ref[d, rows, :] = jnp.broadcast_to(jnp.sum(s, axis=-1, keepdims=True), (L, LANES))
        a_ref[d, rows, :] = jnp.exp2(m - mt)
        e_ref[d, rows, :] = jnp.exp2(-bc - mt)

    def inter(d, c, slot, accumulate):
        off, q_c, kt_c, v_c = chunk_refs(c)
        rows = pl.ds(pl.multiple_of(slot * L, L), L)
        a = a_ref[d, rows, :]
        state = jnp.concatenate([c_ref[d].astype(BF16), n_ref[d].astype(BF16)], axis=1)
        qcn = _dot(q_c, state)
        den = a * qcn[:, dv:] + den_ref[d, rows, :]
        inv = 1.0 / jnp.maximum(jnp.abs(den), e_ref[d, rows, :])
        for k in range(0, dv, LANES):
            h = (a * qcn[:, k:k + LANES] + num_ref[d, rows, k:k + LANES]) * inv
            if accumulate:
                o_ref[pl.ds(off, L), k:k + LANES] += h
            else:
                o_ref[pl.ds(off, L), k:k + LANES] = h
        update_state(d, c, kt_c, v_c)

    for d in range(2):
        for c in (range(ncc) if d == 0 else range(ncc - 1, -1, -1)):
            update_state(d, ncl + c, ktc_ref[:, c * L:(c + 1) * L], vc_ref[c * L:(c + 1) * L, :])

    def run_half(first, accumulate):
        def chunk_of(d, g):
            return first + g if d == 0 else ncl - 1 - first - g

        def intra_step(g, carry):
            for d in range(2):
                intra(d, chunk_of(d, g), g)
            return carry

        def inter_step(g, carry):
            for d in range(2):
                inter(d, chunk_of(d, g), g, accumulate)
            return carry

        lax.fori_loop(0, half, intra_step, 0, unroll=4)
        lax.fori_loop(0, half, inter_step, 0, unroll=4)

    run_half(0, False)
    run_half(half, True)


def _mlstm(q, kt, v, ktc, vc, gl, gc, *, batch, t_lat, t_ctx):
    ncl = t_lat // CHUNK
    ncc = t_ctx // CHUNK
    nc = ncc + ncl
    assert ncl % 4 == 0
    half_rows = t_lat // 2
    dh = M_HEAD_DIM
    kern = functools.partial(_mlstm_kernel, ncc=ncc, ncl=ncl)
    return pl.pallas_call(
        kern,
        grid=(batch, M_HEADS),
        in_specs=[
            pl.BlockSpec((t_lat, dh), lambda b, h: (b, h)),
            pl.BlockSpec((dh, t_lat), lambda b, h: (h, b)),
            pl.BlockSpec((t_lat, dh), lambda b, h: (b, h)),
            pl.BlockSpec((dh, t_ctx), lambda b, h: (h, b)),
            pl.BlockSpec((t_ctx, dh), lambda b, h: (b, h)),
            pl.BlockSpec((4, None, None, ncl, CHUNK), lambda b, h: (0, h, b, 0, 0)),
            pl.BlockSpec((4, None, None, ncc, CHUNK), lambda b, h: (0, h, b, 0, 0)),
        ],
        out_specs=pl.BlockSpec((t_lat, dh), lambda b, h: (b, h)),
        out_shape=jax.ShapeDtypeStruct((batch * t_lat, M_WIDTH), F32),
        scratch_shapes=[
            pltpu.VMEM((2, dh, dh), F32),
            pltpu.VMEM((2, dh, LANES), F32),
            pltpu.VMEM((2, nc, CHUNK), F32),
            pltpu.VMEM((2, nc, CHUNK), F32),
            pltpu.VMEM((2, nc, CHUNK), F32),
            pltpu.VMEM((2, nc, LANES), F32),
            pltpu.VMEM((2, nc, LANES), F32),
            pltpu.VMEM((2, half_rows, dh), F32),
            pltpu.VMEM((2, half_rows, LANES), F32),
            pltpu.VMEM((2, half_rows, LANES), F32),
            pltpu.VMEM((2, half_rows, LANES), F32),
        ],
        compiler_params=_params("parallel", "arbitrary"),
        name="mlstm",
    )(q, kt, v, ktc, vc, gl, gc)


ROW_BLOCK = 4
ROW_SUB = 8
GROUPS_PER_STEP = 2


def _fourier_kernel(x_ref, bd_ref, w2_ref, kc_ref, ks_ref, *refs, rows, scale, side):
    ns = len(side)
    o_ref, zr_ref, zi_ref = refs[ns], refs[2 * ns + 1], refs[2 * ns + 2]
    _run_side_casts(refs[:ns], refs[ns + 1:2 * ns + 1], side)
    gd = F_GROUP_DIM
    nb = ROW_BLOCK * GRID_W
    for blk in range(rows // ROW_BLOCK):
        pq = _dot(bd_ref[...], x_ref[blk * nb:(blk + 1) * nb, :])
        ar = pq[0:nb, :].astype(BF16)
        ai = pq[nb:, :].astype(BF16)
        rs = slice(blk * ROW_BLOCK, (blk + 1) * ROW_BLOCK)
        for g in range(GROUPS_PER_STEP):
            cs = slice(g * gd, (g + 1) * gd)
            z = _dot(jnp.concatenate([ar[:, cs], ai[:, cs]], axis=1), w2_ref[...])
            zr_ref[rs, :, cs] = z[:, 0:gd].reshape(ROW_BLOCK, GRID_W, gd)
            zi_ref[rs, :, cs] = z[:, gd:].reshape(ROW_BLOCK, GRID_W, gd)
    nw = GROUPS_PER_STEP * gd
    for j in range(GRID_W // ROW_SUB):
        js = slice(j * ROW_SUB, (j + 1) * ROW_SUB)
        zr = zr_ref[:, js, :].reshape(rows * ROW_SUB, nw).astype(BF16)
        zi = zi_ref[:, js, :].reshape(rows * ROW_SUB, nw).astype(BF16)
        y = _dot(kc_ref[...], zr) + _dot(ks_ref[...], zi)
        o_ref[:, js, :] = (y * scale).reshape(rows, ROW_SUB, nw)


def _dft_mats(n):
    k = np.arange(n)
    ang = 2.0 * np.pi * np.outer(k, k) / n
    return np.cos(ang), np.sin(ang)


def _fourier(xf, *, batch, t_lat, side_jobs):
    rows = t_lat // GRID_W
    assert rows % ROW_BLOCK == 0
    ck, sk = _dft_mats(F_GROUP_DIM)
    cc, sc = _dft_mats(GRID_W)
    cr, sr = _dft_mats(rows)
    eye_b = np.eye(ROW_BLOCK)
    eye_s = np.eye(ROW_SUB)
    tables = (
        np.concatenate([np.kron(eye_b, cc), np.kron(eye_b, -sc)], axis=0),
        np.block([[ck, -sk], [sk, ck]]),
        np.kron(cr, eye_s), np.kron(sr, eye_s),
    )
    bd, w2, kc, ks = (jnp.asarray(t, F32).astype(BF16) for t in tables)
    scale = 1.0 / math.sqrt(rows * GRID_W * F_GROUP_DIM)
    gsteps = F_GROUPS // GROUPS_PER_STEP
    jobs = side_jobs(batch * gsteps)
    s_in, s_out, s_shape = _side_specs(jobs, lambda b, g: b * gsteps + g)
    kern = functools.partial(_fourier_kernel, rows=rows, scale=scale, side=tuple(j.transpose for j in jobs))
    nw = GROUPS_PER_STEP * F_GROUP_DIM
    out, *side_out = pl.pallas_call(
        kern,
        grid=(batch, gsteps),
        in_specs=[
            pl.BlockSpec((t_lat, nw), lambda b, g: (b, g)),
            _const_spec(bd.shape), _const_spec(w2.shape), _const_spec(kc.shape), _const_spec(ks.shape),
            *s_in,
        ],
        out_specs=[pl.BlockSpec((None, rows, GRID_W, nw), lambda b, g: (b, 0, 0, g)), *s_out],
        out_shape=[jax.ShapeDtypeStruct((batch, rows, GRID_W, F_WIDTH), F32), *s_shape],
        scratch_shapes=[
            pltpu.VMEM((rows, GRID_W, nw), F32),
            pltpu.VMEM((rows, GRID_W, nw), F32),
        ],
        compiler_params=_params("arbitrary", "arbitrary"),
        name="fourier",
    )(xf, bd, w2, kc, ks, *[j.src for j in jobs])
    return (out.reshape(batch * t_lat, F_WIDTH), *side_out)


def _merge_kernel(x_ref, yf_ref, h_ref, o_ref, gf_ref, gm_ref, mod_ref, g_ref, hg_ref,
                  wf_ref, wm_ref, wo_ref, out_ref, hm_ref):
    dh = M_HEAD_DIM
    for hd in range(M_HEADS):
        sl = slice(hd * dh, (hd + 1) * dh)
        hn = _rms(h_ref[:, sl], hg_ref[:, sl])
        hm_ref[:, sl] = (jax.nn.sigmoid(o_ref[:, sl].astype(F32)) * hn).astype(BF16)
    y = (jax.nn.sigmoid(gf_ref[...].astype(F32)) * _dot(yf_ref[...].astype(BF16), wf_ref[...])
         + jax.nn.sigmoid(gm_ref[...].astype(F32)) * _dot(hm_ref[...], wm_ref[...]))
    out = _dot(y.astype(BF16), wo_ref[...])
    out_ref[...] = x_ref[...] + mod_ref[5:6, :] * _rms(out, g_ref[3:4, :])


def _merge(x, yf, h, o, gf, gm, mod, g, hg, wf, wm, wo, *, tm, tiles_per_seq):
    n, d = x.shape
    tok = lambda width: pl.BlockSpec((tm, width), lambda i: (i, 0))
    return pl.pallas_call(
        _merge_kernel,
        grid=(n // tm,),
        in_specs=[
            tok(d), tok(F_WIDTH), tok(M_WIDTH), tok(M_WIDTH), tok(d), tok(d),
            pl.BlockSpec((None, N_ADA, d), lambda i: (i // tiles_per_seq, 0, 0)),
            _const_spec(g.shape), _const_spec(hg.shape),
            _const_spec(wf.shape), _const_spec(wm.shape), _const_spec(wo.shape),
        ],
        out_specs=tok(d),
        out_shape=jax.ShapeDtypeStruct((n, d), F32),
        scratch_shapes=[pltpu.VMEM((tm, M_WIDTH), BF16)],
        compiler_params=_params("parallel"),
        name="merge",
    )(x, yf, h, o, gf, gm, mod, g, hg, wf, wm, wo)


def _tile(t, cap):
    tm = min(t, cap)
    assert t % tm == 0
    return tm


def kernel(x, c, ctx, c_ctx, w_ada, b_ada, norm_g, w13_a, w2_a, w_in, b_in, conv_w, conv_b,
           head_g, w_four, w_mproj, w_out, w13_b, w2_b):
    batch, t_lat, d = x.shape
    t_ctx = ctx.shape[1]
    assert d == D_MODEL and w_ada.shape[0] == 1, "single-layer kernel"
    assert batch + 1 <= MOD_ROWS and t_lat % (GRID_W * CHUNK // math.gcd(GRID_W, CHUNK)) == 0
    assert t_ctx % CHUNK == 0

    cvec = jnp.concatenate([c, c_ctx[None], jnp.zeros((MOD_ROWS - batch - 1, d), F32)], axis=0)
    mod = _ada(cvec, w_ada[0], b_ada[0][None])
    g = norm_g[0]

    gpad = LANES - N_GATES
    bi = b_in[0][None]
    b_p = (bi[:, :COL_GATES], jnp.pad(bi[:, COL_GATES:COL_BR], ((0, 0), (0, gpad))), bi[:, COL_BR:])
    w13a, w2a = w13_a[0].astype(BF16), w2_a[0].astype(BF16)
    cw, cb, hg = conv_w[0], conv_b[0][None], head_g[0][None]
    w_in_jobs = functools.partial(_w_in_jobs, w_in[0].T)
    late_jobs = lambda steps: tuple(_row_cast(w[0], steps) for w in (w13_b, w2_b, w_four, w_mproj, w_out))

    tm_l = _tile(t_lat, 512)
    tm_c = _tile(t_ctx, 512)
    tps_l = t_lat // tm_l
    tps_c = t_ctx // tm_c
    xl = x.reshape(batch * t_lat, d)
    xc = ctx.reshape(batch * t_ctx, d)

    lat = dict(tm=tm_l, tiles_per_seq=tps_l, mod_row0=0, mod_step=1)
    con = dict(tm=tm_c, tiles_per_seq=tps_c, mod_row0=batch, mod_step=0)
    tm_f = _tile(t_lat, 2 * FFN_SUB)
    lat_ffn = dict(lat, tm=tm_f, tiles_per_seq=t_lat // tm_f)
    con_ffn = dict(con, tm=_tile(batch * t_ctx, 2 * FFN_SUB))

    hl, *w_p = _ffn(xl, mod, g, w13a, w2a, mod0=0, g0=0, name="ffn_a_lat", side_jobs=w_in_jobs, **lat_ffn)
    hc = _ffn(xc, mod, g, w13a, w2a, mod0=0, g0=0, name="ffn_a_ctx", **con_ffn)

    xf, q, kt, v, o, gt, gf, gm = _proj(hl, mod, g, w_p, b_p, cw, cb, full=True, name="proj_lat", **lat)
    ktc, vc, gtc = _proj(hc, mod, g, w_p, b_p, cw, cb, full=False, name="proj_ctx", **con)

    gl = gt.reshape(4, M_HEADS, batch, t_lat // CHUNK, CHUNK)
    gc = gtc.reshape(4, M_HEADS, batch, t_ctx // CHUNK, CHUNK)
    hm = _mlstm(q, kt, v, ktc, vc, gl, gc, batch=batch, t_lat=t_lat, t_ctx=t_ctx)
    yf, w13b, w2b, wf, wm, wo = _fourier(xf, batch=batch, t_lat=t_lat, side_jobs=late_jobs)

    hl = _merge(hl, yf, hm, o, gf, gm, mod, g, hg, wf, wm, wo, tm=tm_l, tiles_per_seq=tps_l)
    hl = _ffn(hl, mod, g, w13b, w2b, mod0=6, g0=4, name="ffn_b_lat", **lat_ffn)
    return hl.reshape(batch, t_lat, d)
```

```python
import functools
import math
from typing import Any, Callable, NamedTuple

import numpy as np
import jax
import jax.numpy as jnp
from jax import lax
from jax.experimental import pallas as pl
from jax.experimental.pallas import tpu as pltpu

D_MODEL = 1024
GRID_W = 64
FF_HALF = 2816
F_GROUPS = 4
F_GROUP_DIM = 128
F_WIDTH = F_GROUPS * F_GROUP_DIM
M_HEADS = 4
M_HEAD_DIM = 256
M_WIDTH = M_HEADS * M_HEAD_DIM
CONV_K = 3
CHUNK = 128
N_ADA = 9
EPS = 1e-6
LOG2E = math.log2(math.e)

COL_Q = F_WIDTH
COL_K = COL_Q + M_WIDTH
COL_V = COL_K + M_WIDTH
COL_O = COL_V + M_WIDTH
COL_GATES = COL_O + M_WIDTH
N_GATES = 4 * M_HEADS
COL_BR = COL_GATES + N_GATES

LANES = 128
F32_SUBLANES = 8
BF16_SUBLANES = 16
MOD_ROWS = 8
HALO = BF16_SUBLANES
VMEM_LIMIT = 56 * 1024 * 1024

F32 = jnp.float32
BF16 = jnp.bfloat16


def _const_spec(shape):
    nd = len(shape)
    return pl.BlockSpec(shape, lambda *_: (0,) * nd, pipeline_mode=pl.Buffered(1))


def _params(*sem):
    return pltpu.CompilerParams(dimension_semantics=sem, vmem_limit_bytes=VMEM_LIMIT)


def _rms(x, g):
    return x * lax.rsqrt(jnp.mean(x * x, axis=-1, keepdims=True) + EPS) * g


def _dot(a, b):
    return jnp.dot(a, b, preferred_element_type=F32)


class _SideCast(NamedTuple):
    src: Any
    in_block: tuple
    in_index: Callable[[Any], tuple]
    out_shape: tuple
    out_block: tuple
    out_index: Callable[[Any], tuple]
    transpose: bool


def _num_blocks(units, steps):
    return max(k for k in range(1, min(units, steps) + 1) if units % k == 0)


def _row_cast(w, steps):
    rows, cols = w.shape
    nb = _num_blocks(rows // BF16_SUBLANES, steps)
    rb = rows // nb
    idx = lambda s: (jnp.minimum(s, nb - 1), 0)
    return _SideCast(w, (rb, cols), idx, (rows, cols), (rb, cols), idx, False)


def _transposed_cast(wt, row0, n, steps):
    nb = _num_blocks(n // LANES, steps)
    cb = n // nb
    d = wt.shape[1]
    assert row0 % F32_SUBLANES == 0
    row = lambda s: pl.multiple_of(row0 + jnp.minimum(s, nb - 1) * cb, F32_SUBLANES)
    return _SideCast(wt, (pl.Element(cb), pl.Element(d)), lambda s: (row(s), 0),
                     (d, n), (d, cb), lambda s: (0, jnp.minimum(s, nb - 1)), True)


def _side_specs(jobs, step_of):
    ins = [pl.BlockSpec(j.in_block, lambda *g, j=j: j.in_index(step_of(*g))) for j in jobs]
    outs = [pl.BlockSpec(j.out_block, lambda *g, j=j: j.out_index(step_of(*g))) for j in jobs]
    shapes = [jax.ShapeDtypeStruct(j.out_shape, BF16) for j in jobs]
    return ins, outs, shapes


def _run_side_casts(in_refs, out_refs, transposes):
    for i_ref, o_ref, transpose in zip(in_refs, out_refs, transposes):
        v = i_ref[...]
        o_ref[...] = (v.T if transpose else v).astype(BF16)


def _ada_kernel(c_ref, w_ref, b_ref, *refs, side):
    ns = len(side)
    _run_side_casts(refs[:ns], refs[ns + 1:], side)
    c = c_ref[...]
    s = (c * jax.nn.sigmoid(c)).astype(BF16)
    refs[ns][...] = _dot(s, w_ref[...].astype(BF16)) + b_ref[...]


def _ada(cvec, w, b, side_jobs):
    d = cvec.shape[1]
    jobs = side_jobs(N_ADA)
    s_in, s_out, s_shape = _side_specs(jobs, lambda j: j)
    out, *side_out = pl.pallas_call(
        functools.partial(_ada_kernel, side=tuple(j.transpose for j in jobs)),
        grid=(N_ADA,),
        in_specs=[
            pl.BlockSpec((MOD_ROWS, d), lambda j: (0, 0)),
            pl.BlockSpec((d, d), lambda j: (0, j)),
            pl.BlockSpec((1, d), lambda j: (0, j)),
            *s_in,
        ],
        out_specs=[pl.BlockSpec((MOD_ROWS, d), lambda j: (0, j)), *s_out],
        out_shape=[jax.ShapeDtypeStruct((MOD_ROWS, N_ADA * d), F32), *s_shape],
        compiler_params=_params("arbitrary"),
        name="ada",
    )(cvec, w, b, *[j.src for j in jobs])
    return (out.reshape(MOD_ROWS, N_ADA, d), *side_out)


FF_CHUNK = 256
FFN_SUB = 512


def _ffn_kernel(x_ref, mod_ref, g_ref, w13_ref, w2_ref, *refs, mod0, g0, side):
    ns = len(side)
    o_ref, hid_ref = refs[ns], refs[2 * ns + 1]
    _run_side_casts(refs[:ns], refs[ns + 1:2 * ns + 1], side)
    shift = mod_ref[mod0:mod0 + 1, :]
    gain_pre = g_ref[g0:g0 + 1, :] * (1.0 + mod_ref[mod0 + 1:mod0 + 2, :])
    gain_post = 0.5 * mod_ref[mod0 + 2:mod0 + 3, :] * g_ref[g0 + 1:g0 + 2, :]
    nsub, sub, _ = hid_ref.shape
    rows = lambda s: slice(s * sub, (s + 1) * sub)

    for s in range(nsub):
        x = x_ref[rows(s), :]
        u = (_rms(x, gain_pre) + shift).astype(BF16)
        for c in range(FF_HALF // FF_CHUNK):
            lo = c * FF_CHUNK
            a = _dot(u, w13_ref[:, lo:lo + FF_CHUNK])
            b = _dot(u, w13_ref[:, FF_HALF + lo:FF_HALF + lo + FF_CHUNK])
            hid_ref[s, :, lo:lo + FF_CHUNK] = (a * jax.nn.sigmoid(a) * b).astype(BF16)
        y = _dot(hid_ref[s], w2_ref[...])
        o_ref[rows(s), :] = x + _rms(y, gain_post)


def _ffn(h, mod, g, w13, w2, *, tm, tiles_per_seq, mod_row0, mod_step, mod0, g0, name, side_jobs=None):
    n, d = h.shape
    jobs = side_jobs(n // tm) if side_jobs else ()
    s_in, s_out, s_shape = _side_specs(jobs, lambda i: i)
    kern = functools.partial(_ffn_kernel, mod0=mod0, g0=g0, side=tuple(j.transpose for j in jobs))
    sub = min(tm, FFN_SUB)
    outs = pl.pallas_call(
        kern,
        grid=(n // tm,),
        in_specs=[
            pl.BlockSpec((tm, d), lambda i: (i, 0)),
            pl.BlockSpec((None, N_ADA, d), lambda i: (mod_row0 + mod_step * (i // tiles_per_seq), 0, 0)),
            _const_spec(g.shape),
            _const_spec(w13.shape),
            _const_spec(w2.shape),
            *s_in,
        ],
        out_specs=[pl.BlockSpec((tm, d), lambda i: (i, 0)), *s_out],
        out_shape=[jax.ShapeDtypeStruct((n, d), F32), *s_shape],
        scratch_shapes=[pltpu.VMEM((tm // sub, sub, FF_HALF), BF16)],
        compiler_params=_params("arbitrary"),
        name=name,
    )(h, mod, g, w13, w2, *[j.src for j in jobs])
    return outs if jobs else outs[0]


QK_CHUNK = 256


def _w_in_jobs(wt, steps):
    return (_transposed_cast(wt, 0, COL_GATES, steps),
            _transposed_cast(wt, COL_GATES, LANES, steps),
            _transposed_cast(wt, COL_BR, wt.shape[0] - COL_BR, steps))


def _proj_kernel(x_ref, xp_ref, xn_ref, mod_ref, g_ref, w_ref, wg_ref, wbr_ref, b_ref, bg_ref, bbr_ref,
                 cw_ref, cb_ref, *refs, tm, tiles_per_seq, full):
    if full:
        xf_ref, q_ref, kt_ref, v_ref, o_ref, gt_ref, gf_ref, gm_ref, u_ref, p_ref = refs
    else:
        kt_ref, v_ref, gt_ref, u_ref, p_ref = refs
    i = pl.program_id(0)
    first = (i % tiles_per_seq) == 0
    last = (i % tiles_per_seq) == tiles_per_seq - 1
    shift = mod_ref[3:4, :]
    gain = g_ref[2:3, :] * (1.0 + mod_ref[4:5, :])

    def mk_u(x):
        return (_rms(x, gain) + shift).astype(BF16)

    u_ref[0:HALO, :] = mk_u(xp_ref[...])
    u_ref[HALO:HALO + tm, :] = mk_u(x_ref[...])
    u_ref[HALO + tm:, :] = mk_u(xn_ref[...])
    u = u_ref[HALO:HALO + tm, :]

    def lin(dst_ref, src_ref, bias_ref, lo, k):
        cols = slice(lo + k, lo + k + QK_CHUNK)
        dst_ref[:, k:k + QK_CHUNK] = (_dot(u, src_ref[:, cols]) + bias_ref[:, cols]).astype(dst_ref.dtype)

    def gates():
        gt_ref[...] = (_dot(u, wg_ref[...]) + bg_ref[...]).T[0:N_GATES, :]

    top = slice(HALO - F32_SUBLANES, HALO)
    bot = slice(HALO + tm, HALO + tm + F32_SUBLANES)

    def conv(c):
        lo = c * QK_CHUNK
        pb = p_ref.at[c % 2]
        p = _dot(u_ref[...], w_ref[:, COL_Q + lo:COL_Q + lo + QK_CHUNK]) + b_ref[:, COL_Q + lo:COL_Q + lo + QK_CHUNK]
        pb[...] = p
        pb[top, :] = jnp.where(first, 0.0, p[top, :])
        pb[bot, :] = jnp.where(last, 0.0, p[bot, :])
        cw = cw_ref[:, lo:lo + QK_CHUNK]
        y = (cw[0:1, :] * pb[HALO - 1:HALO - 1 + tm, :] + cw[1:2, :] * pb[HALO:HALO + tm, :]
             + cw[2:3, :] * pb[HALO + 1:HALO + 1 + tm, :]) + cb_ref[:, lo:lo + QK_CHUNK]
        y = y * jax.nn.sigmoid(y)
        if lo < M_WIDTH:
            q_ref[:, lo:lo + QK_CHUNK] = y.astype(BF16)
        else:
            kt_ref[lo - M_WIDTH:lo - M_WIDTH + QK_CHUNK, :] = (y * (M_HEAD_DIM ** -0.5)).T.astype(BF16)

    chunks = lambda width: range(0, width, QK_CHUNK)
    plain = [functools.partial(lin, v_ref, w_ref, b_ref, COL_V, k) for k in chunks(M_WIDTH)] + [gates]
    if full:
        plain += [functools.partial(lin, xf_ref, w_ref, b_ref, 0, k) for k in chunks(F_WIDTH)]
        plain += [functools.partial(lin, o_ref, w_ref, b_ref, COL_O, k) for k in chunks(M_WIDTH)]
        plain += [functools.partial(lin, gf_ref, wbr_ref, bbr_ref, 0, k) for k in chunks(D_MODEL)]
        plain += [functools.partial(lin, gm_ref, wbr_ref, bbr_ref, D_MODEL, k) for k in chunks(D_MODEL)]
    convs = list(range(0 if full else M_WIDTH // QK_CHUNK, 2 * M_WIDTH // QK_CHUNK))
    per_conv = -(-len(plain) // len(convs))
    for n_done, c in enumerate(convs):
        conv(c)
        for task in plain[n_done * per_conv:(n_done + 1) * per_conv]:
            task()


def _proj(h, mod, g, ws, bs, cw, cb, *, tm, tiles_per_seq, mod_row0, mod_step, full, name):
    n, d = h.shape
    hb = tm // HALO
    nhb = n // HALO
    kern = functools.partial(_proj_kernel, tm=tm, tiles_per_seq=tiles_per_seq, full=full)
    tok = lambda width, dt: (pl.BlockSpec((tm, width), lambda i: (i, 0)), jax.ShapeDtypeStruct((n, width), dt))
    tr = lambda width, dt: (pl.BlockSpec((width, tm), lambda i: (0, i)), jax.ShapeDtypeStruct((width, n), dt))
    xf, q, o, gf, gm = tok(F_WIDTH, BF16), tok(M_WIDTH, BF16), tok(M_WIDTH, BF16), tok(d, BF16), tok(d, BF16)
    kt = tr(M_WIDTH, BF16)
    v = tok(M_WIDTH, BF16)
    gt = tr(N_GATES, F32)
    outs = (xf, q, kt, v, o, gt, gf, gm) if full else (kt, v, gt)
    return pl.pallas_call(
        kern,
        grid=(n // tm,),
        in_specs=[
            pl.BlockSpec((tm, d), lambda i: (i, 0)),
            pl.BlockSpec((HALO, d), lambda i: (jnp.maximum(i * hb - 1, 0), 0)),
            pl.BlockSpec((HALO, d), lambda i: (jnp.minimum((i + 1) * hb, nhb - 1), 0)),
            pl.BlockSpec((None, N_ADA, d), lambda i: (mod_row0 + mod_step * (i // tiles_per_seq), 0, 0)),
            _const_spec(g.shape),
            *[_const_spec(a.shape) for a in (*ws, *bs, cw, cb)],
        ],
        out_specs=[s for s, _ in outs],
        out_shape=[s for _, s in outs],
        scratch_shapes=[pltpu.VMEM((tm + 2 * HALO, d), BF16),
                        pltpu.VMEM((2, tm + 2 * HALO, QK_CHUNK), F32)],
        compiler_params=_params("parallel"),
        name=name,
    )(h, h, h, mod, g, *ws, *bs, cw, cb)


def _lane_scan(x, reverse):
    lane = lax.broadcasted_iota(jnp.int32, x.shape, 1)
    k = 1
    while k < CHUNK:
        if reverse:
            x = x + jnp.where(lane < CHUNK - k, pltpu.roll(x, CHUNK - k, 1), 0.0)
        else:
            x = x + jnp.where(lane >= k, pltpu.roll(x, k, 1), 0.0)
        k *= 2
    return x


def _mlstm_kernel(q_ref, kt_ref, v_ref, ktc_ref, vc_ref, gl_ref, gc_ref, o_ref,
                  c_ref, n_ref, r_ref, wk_ref, lf_ref, ms_ref, dec_ref,
                  num_ref, den_ref, a_ref, e_ref, *, ncc, ncl):
    L = CHUNK
    dv = M_HEAD_DIM
    half = ncl // 2
    t_idx = lax.broadcasted_iota(jnp.int32, (L, L), 0)
    s_idx = lax.broadcasted_iota(jnp.int32, (L, L), 1)
    masks = (s_idx <= t_idx, s_idx >= t_idx)

    def gate_rows(g_ref, d):
        li = g_ref[2 * d] * LOG2E
        gf = g_ref[2 * d + 1]
        lf = (jnp.minimum(gf, 0.0) - jnp.log1p(jnp.exp(-jnp.abs(gf)))) * LOG2E
        bcum = _lane_scan(lf, reverse=(d == 1))
        r = li - bcum
        rmax = jnp.max(r, axis=-1, keepdims=True)
        b_end = bcum[:, L - 1:L] if d == 0 else bcum[:, 0:1]
        return lf, r, rmax, b_end

    for d in range(2):
        parts = (gate_rows(gl_ref, d), gate_rows(gc_ref, d))
        rev = d == 1
        order = [(1, c) for c in (range(ncc - 1, -1, -1) if rev else range(ncc))]
        order += [(0, c) for c in (range(ncl - 1, -1, -1) if rev else range(ncl))]
        m = jnp.zeros((1, 1), F32)
        for part, c in order:
            row = c + (ncl if part == 1 else 0)
            ms_ref[d, row:row + 1, :] = jnp.broadcast_to(m, (1, LANES))
            _, _, rmax, b_end = parts[part]
            m = b_end[c:c + 1, :] + jnp.maximum(m, rmax[c:c + 1, :])
        for part, (lo, cnt) in enumerate(((0, ncl), (ncl, ncc))):
            lf, r, rmax, _ = parts[part]
            m_start = ms_ref[d, lo:lo + cnt, 0:1]
            m_end = jnp.maximum(m_start, rmax)
            dec_ref[d, lo:lo + cnt, :] = jnp.broadcast_to(jnp.exp2(m_start - m_end), (cnt, LANES))
            wk_ref[d, lo:lo + cnt, :] = jnp.exp2(r - m_end)
            r_ref[d, lo:lo + cnt, :] = r
            lf_ref[d, lo:lo + cnt, :] = lf

    c_ref[...] = jnp.zeros_like(c_ref)
    n_ref[...] = jnp.zeros_like(n_ref)

    def update_state(d, row, kt_c, v_c):
        wk = wk_ref[d, pl.ds(row, 1), :]
        dec = dec_ref[d, pl.ds(row, 1), :]
        ktw = kt_c.astype(F32) * wk
        c_ref[d] = jnp.concatenate([dec] * (dv // LANES), axis=1) * c_ref[d] + _dot(ktw.astype(BF16), v_c)
        n_ref[d] = dec * n_ref[d] + jnp.sum(ktw, axis=-1, keepdims=True)

    def chunk_refs(c):
        off = pl.multiple_of(c * L, L)
        return off, q_ref[pl.ds(off, L), :], kt_ref[:, pl.ds(off, L)], v_ref[pl.ds(off, L), :]

    def intra(d, c, slot):
        _, q_c, kt_c, v_c = chunk_refs(c)
        rows = pl.ds(pl.multiple_of(slot * L, L), L)
        r_row = r_ref[d, pl.ds(c, 1), :]
        lf_row = lf_ref[d, pl.ds(c, 1), :]
        m = ms_ref[d, pl.ds(c, 1), :]
        rm = jnp.where(masks[d], r_row, -jnp.inf)
        mt = jnp.maximum(m, jnp.max(rm, axis=-1, keepdims=True))
        bc = jnp.sum(jnp.where(masks[d], lf_row, 0.0), axis=-1, keepdims=True)
        s = _dot(q_c, kt_c) * jnp.exp2(rm - mt)
        num_ref[d, rows, :] = _dot(s.astype(BF16), v_c)
        den_ref[d, rows, :] = jnp.broadcast_to(jnp.sum(s, axis=-1, keepdims=True), (L, LANES))
        a_ref[d, rows, :] = jnp.exp2(m - mt)
        e_ref[d, rows, :] = jnp.exp2(-bc - mt)

    def inter(d, c, slot, accumulate):
        off, q_c, kt_c, v_c = chunk_refs(c)
        rows = pl.ds(pl.multiple_of(slot * L, L), L)
        a = a_ref[d, rows, :]
        state = jnp.concatenate([c_ref[d].astype(BF16), n_ref[d].astype(BF16)], axis=1)
        qcn = _dot(q_c, state)
        den = a * qcn[:, dv:] + den_ref[d, rows, :]
        inv = 1.0 / jnp.maximum(jnp.abs(den), e_ref[d, rows, :])
        for k in range(0, dv, LANES):
            h = (a * qcn[:, k:k + LANES] + num_ref[d, rows, k:k + LANES]) * inv
            if accumulate:
                o_ref[pl.ds(off, L), k:k + LANES] += h
            else:
                o_ref[pl.ds(off, L), k:k + LANES] = h
        update_state(d, c, kt_c, v_c)

    for d in range(2):
        for c in (range(ncc) if d == 0 else range(ncc - 1, -1, -1)):
            update_state(d, ncl + c, ktc_ref[:, c * L:(c + 1) * L], vc_ref[c * L:(c + 1) * L, :])

    def run_half(first, accumulate):
        def chunk_of(d, g):
            return first + g if d == 0 else ncl - 1 - first - g

        def intra_step(g, carry):
            for d in range(2):
                intra(d, chunk_of(d, g), g)
            return carry

        def inter_step(g, carry):
            for d in range(2):
                inter(d, chunk_of(d, g), g, accumulate)
            return carry

        lax.fori_loop(0, half, intra_step, 0, unroll=4)
        lax.fori_loop(0, half, inter_step, 0, unroll=4)

    run_half(0, False)
    run_half(half, True)


def _mlstm(q, kt, v, ktc, vc, gl, gc, *, batch, t_lat, t_ctx):
    ncl = t_lat // CHUNK
    ncc = t_ctx // CHUNK
    nc = ncc + ncl
    assert ncl % 4 == 0
    slot_rows = t_lat // 2
    dh = M_HEAD_DIM
    kern = functools.partial(_mlstm_kernel, ncc=ncc, ncl=ncl)
    return pl.pallas_call(
        kern,
        grid=(batch, M_HEADS),
        in_specs=[
            pl.BlockSpec((t_lat, dh), lambda b, h: (b, h)),
            pl.BlockSpec((dh, t_lat), lambda b, h: (h, b)),
            pl.BlockSpec((t_lat, dh), lambda b, h: (b, h)),
            pl.BlockSpec((dh, t_ctx), lambda b, h: (h, b)),
            pl.BlockSpec((t_ctx, dh), lambda b, h: (b, h)),
            pl.BlockSpec((4, None, None, ncl, CHUNK), lambda b, h: (0, h, b, 0, 0)),
            pl.BlockSpec((4, None, None, ncc, CHUNK), lambda b, h: (0, h, b, 0, 0)),
        ],
        out_specs=pl.BlockSpec((t_lat, dh), lambda b, h: (b, h)),
        out_shape=jax.ShapeDtypeStruct((batch * t_lat, M_WIDTH), F32),
        scratch_shapes=[
            pltpu.VMEM((2, dh, dh), F32),
            pltpu.VMEM((2, dh, LANES), F32),
            pltpu.VMEM((2, nc, CHUNK), F32),
            pltpu.VMEM((2, nc, CHUNK), F32),
            pltpu.VMEM((2, nc, CHUNK), F32),
            pltpu.VMEM((2, nc, LANES), F32),
            pltpu.VMEM((2, nc, LANES), F32),
            pltpu.VMEM((2, slot_rows, dh), F32),
            pltpu.VMEM((2, slot_rows, LANES), F32),
            pltpu.VMEM((2, slot_rows, LANES), F32),
            pltpu.VMEM((2, slot_rows, LANES), F32),
        ],
        compiler_params=_params("parallel", "arbitrary"),
        name="mlstm",
    )(q, kt, v, ktc, vc, gl, gc)


ROW_BLOCK = 4
ROW_SUB = 8
GROUPS_PER_STEP = 2


def _fourier_kernel(x_ref, bd_ref, w2_ref, kc_ref, ks_ref, *refs, rows, scale, side):
    ns = len(side)
    o_ref, zr_ref, zi_ref = refs[ns], refs[2 * ns + 1], refs[2 * ns + 2]
    _run_side_casts(refs[:ns], refs[ns + 1:2 * ns + 1], side)
    gd = F_GROUP_DIM
    nb = ROW_BLOCK * GRID_W
    for blk in range(rows // ROW_BLOCK):
        pq = _dot(bd_ref[...], x_ref[blk * nb:(blk + 1) * nb, :])
        ar = pq[0:nb, :].astype(BF16)
        ai = pq[nb:, :].astype(BF16)
        rs = slice(blk * ROW_BLOCK, (blk + 1) * ROW_BLOCK)
        for g in range(GROUPS_PER_STEP):
            cs = slice(g * gd, (g + 1) * gd)
            z = _dot(jnp.concatenate([ar[:, cs], ai[:, cs]], axis=1), w2_ref[...])
            zr_ref[rs, :, cs] = z[:, 0:gd].reshape(ROW_BLOCK, GRID_W, gd)
            zi_ref[rs, :, cs] = z[:, gd:].reshape(ROW_BLOCK, GRID_W, gd)
    nw = GROUPS_PER_STEP * gd
    for j in range(GRID_W // ROW_SUB):
        js = slice(j * ROW_SUB, (j + 1) * ROW_SUB)
        zr = zr_ref[:, js, :].reshape(rows * ROW_SUB, nw).astype(BF16)
        zi = zi_ref[:, js, :].reshape(rows * ROW_SUB, nw).astype(BF16)
        y = _dot(kc_ref[...], zr) + _dot(ks_ref[...], zi)
        o_ref[:, js, :] = (y * scale).reshape(rows, ROW_SUB, nw)


def _dft_mats(n):
    k = np.arange(n)
    ang = 2.0 * np.pi * np.outer(k, k) / n
    return np.cos(ang), np.sin(ang)


def _fourier(xf, *, batch, t_lat, side_jobs):
    rows = t_lat // GRID_W
    assert rows % ROW_BLOCK == 0
    ck, sk = _dft_mats(F_GROUP_DIM)
    cc, sc = _dft_mats(GRID_W)
    cr, sr = _dft_mats(rows)
    eye_b = np.eye(ROW_BLOCK)
    eye_s = np.eye(ROW_SUB)
    tables = (
        np.concatenate([np.kron(eye_b, cc), np.kron(eye_b, -sc)], axis=0),
        np.block([[ck, -sk], [sk, ck]]),
        np.kron(cr, eye_s), np.kron(sr, eye_s),
    )
    bd, w2, kc, ks = (jnp.asarray(t, F32).astype(BF16) for t in tables)
    scale = 1.0 / math.sqrt(rows * GRID_W * F_GROUP_DIM)
    gsteps = F_GROUPS // GROUPS_PER_STEP
    jobs = side_jobs(batch * gsteps)
    s_in, s_out, s_shape = _side_specs(jobs, lambda b, g: b * gsteps + g)
    kern = functools.partial(_fourier_kernel, rows=rows, scale=scale, side=tuple(j.transpose for j in jobs))
    nw = GROUPS_PER_STEP * F_GROUP_DIM
    out, *side_out = pl.pallas_call(
        kern,
        grid=(batch, gsteps),
        in_specs=[
            pl.BlockSpec((t_lat, nw), lambda b, g: (b, g)),
            _const_spec(bd.shape), _const_spec(w2.shape), _const_spec(kc.shape), _const_spec(ks.shape),
            *s_in,
        ],
        out_specs=[pl.BlockSpec((None, rows, GRID_W, nw), lambda b, g: (b, 0, 0, g)), *s_out],
        out_shape=[jax.ShapeDtypeStruct((batch, rows, GRID_W, F_WIDTH), F32), *s_shape],
        scratch_shapes=[
            pltpu.VMEM((rows, GRID_W, nw), F32),
            pltpu.VMEM((rows, GRID_W, nw), F32),
        ],
        compiler_params=_params("arbitrary", "arbitrary"),
        name="fourier",
    )(xf, bd, w2, kc, ks, *[j.src for j in jobs])
    return (out.reshape(batch * t_lat, F_WIDTH), *side_out)


def _merge_kernel(x_ref, yf_ref, h_ref, o_ref, gf_ref, gm_ref, mod_ref, g_ref, hg_ref,
                  wf_ref, wm_ref, wo_ref, out_ref, hm_ref):
    dh = M_HEAD_DIM
    for hd in range(M_HEADS):
        sl = slice(hd * dh, (hd + 1) * dh)
        hn = _rms(h_ref[:, sl], hg_ref[:, sl])
        hm_ref[:, sl] = (jax.nn.sigmoid(o_ref[:, sl].astype(F32)) * hn).astype(BF16)
    y = (jax.nn.sigmoid(gf_ref[...].astype(F32)) * _dot(yf_ref[...].astype(BF16), wf_ref[...])
         + jax.nn.sigmoid(gm_ref[...].astype(F32)) * _dot(hm_ref[...], wm_ref[...]))
    out = _dot(y.astype(BF16), wo_ref[...])
    out_ref[...] = x_ref[...] + _rms(out, mod_ref[5:6, :] * g_ref[3:4, :])


def _merge(x, yf, h, o, gf, gm, mod, g, hg, wf, wm, wo, *, tm, tiles_per_seq):
    n, d = x.shape
    tok = lambda width: pl.BlockSpec((tm, width), lambda i: (i, 0))
    return pl.pallas_call(
        _merge_kernel,
        grid=(n // tm,),
        in_specs=[
            tok(d), tok(F_WIDTH), tok(M_WIDTH), tok(M_WIDTH), tok(d), tok(d),
            pl.BlockSpec((None, N_ADA, d), lambda i: (i // tiles_per_seq, 0, 0)),
            _const_spec(g.shape), _const_spec(hg.shape),
            _const_spec(wf.shape), _const_spec(wm.shape), _const_spec(wo.shape),
        ],
        out_specs=tok(d),
        out_shape=jax.ShapeDtypeStruct((n, d), F32),
        scratch_shapes=[pltpu.VMEM((tm, M_WIDTH), BF16)],
        compiler_params=_params("parallel"),
        name="merge",
    )(x, yf, h, o, gf, gm, mod, g, hg, wf, wm, wo)


def _tile(t, cap):
    tm = min(t, cap)
    assert t % tm == 0
    return tm


def kernel(x, c, ctx, c_ctx, w_ada, b_ada, norm_g, w13_a, w2_a, w_in, b_in, conv_w, conv_b,
           head_g, w_four, w_mproj, w_out, w13_b, w2_b):
    batch, t_lat, d = x.shape
    t_ctx = ctx.shape[1]
    assert d == D_MODEL and w_ada.shape[0] == 1, "single-layer kernel"
    assert batch + 1 <= MOD_ROWS and t_lat % (GRID_W * CHUNK // math.gcd(GRID_W, CHUNK)) == 0
    assert t_ctx % CHUNK == 0

    cvec = jnp.concatenate([c, c_ctx[None], jnp.zeros((MOD_ROWS - batch - 1, d), F32)], axis=0)
    first_jobs = lambda steps: (_row_cast(w13_a[0], steps), _row_cast(w2_a[0], steps))
    mod, w13a, w2a = _ada(cvec, w_ada[0], b_ada[0][None], first_jobs)
    g = norm_g[0]

    gpad = LANES - N_GATES
    bi = b_in[0][None]
    b_p = (bi[:, :COL_GATES], jnp.pad(bi[:, COL_GATES:COL_BR], ((0, 0), (0, gpad))), bi[:, COL_BR:])
    cw, cb, hg = conv_w[0], conv_b[0][None], head_g[0][None]
    w_in_jobs = functools.partial(_w_in_jobs, w_in[0].T)
    late_jobs = lambda steps: tuple(_row_cast(w[0], steps) for w in (w13_b, w2_b, w_four, w_mproj, w_out))

    tm_l = _tile(t_lat, 512)
    tm_c = _tile(t_ctx, 512)
    tps_l = t_lat // tm_l
    tps_c = t_ctx // tm_c
    xl = x.reshape(batch * t_lat, d)
    xc = ctx.reshape(batch * t_ctx, d)

    lat = dict(tm=tm_l, tiles_per_seq=tps_l, mod_row0=0, mod_step=1)
    con = dict(tm=tm_c, tiles_per_seq=tps_c, mod_row0=batch, mod_step=0)
    tm_f = _tile(t_lat, 2 * FFN_SUB)
    lat_ffn = dict(lat, tm=tm_f, tiles_per_seq=t_lat // tm_f)
    con_ffn = dict(con, tm=_tile(batch * t_ctx, 2 * FFN_SUB))

    hl, *w_p = _ffn(xl, mod, g, w13a, w2a, mod0=0, g0=0, name="ffn_a_lat", side_jobs=w_in_jobs, **lat_ffn)
    hc = _ffn(xc, mod, g, w13a, w2a, mod0=0, g0=0, name="ffn_a_ctx", **con_ffn)

    xf, q, kt, v, o, gt, gf, gm = _proj(hl, mod, g, w_p, b_p, cw, cb, full=True, name="proj_lat", **lat)
    ktc, vc, gtc = _proj(hc, mod, g, w_p, b_p, cw, cb, full=False, name="proj_ctx", **con)

    gl = gt.reshape(4, M_HEADS, batch, t_lat // CHUNK, CHUNK)
    gc = gtc.reshape(4, M_HEADS, batch, t_ctx // CHUNK, CHUNK)
    hm = _mlstm(q, kt, v, ktc, vc, gl, gc, batch=batch, t_lat=t_lat, t_ctx=t_ctx)
    yf, w13b, w2b, wf, wm, wo = _fourier(xf, batch=batch, t_lat=t_lat, side_jobs=late_jobs)

    hl = _merge(hl, yf, hm, o, gf, gm, mod, g, hg, wf, wm, wo, tm=tm_l, tiles_per_seq=tps_l)
    hl = _ffn(hl, mod, g, w13b, w2b, mod0=6, g0=4, name="ffn_b_lat", **lat_ffn)
    return hl.reshape(batch, t_lat, d)
```

```python
import functools
import math
from typing import Any, Callable, NamedTuple

import numpy as np
import jax
import jax.numpy as jnp
from jax import lax
from jax.experimental import pallas as pl
from jax.experimental.pallas import tpu as pltpu

D_MODEL = 1024
GRID_W = 64
FF_HALF = 2816
F_GROUPS = 4
F_GROUP_DIM = 128
F_WIDTH = F_GROUPS * F_GROUP_DIM
M_HEADS = 4
M_HEAD_DIM = 256
M_WIDTH = M_HEADS * M_HEAD_DIM
CONV_K = 3
CHUNK = 128
N_ADA = 9
EPS = 1e-6
LOG2E = math.log2(math.e)

COL_Q = F_WIDTH
COL_K = COL_Q + M_WIDTH
COL_V = COL_K + M_WIDTH
COL_O = COL_V + M_WIDTH
COL_GATES = COL_O + M_WIDTH
N_GATES = 4 * M_HEADS
COL_BR = COL_GATES + N_GATES

LANES = 128
F32_SUBLANES = 8
BF16_SUBLANES = 16
MOD_ROWS = 8
HALO = BF16_SUBLANES
VMEM_LIMIT = 56 * 1024 * 1024

F32 = jnp.float32
BF16 = jnp.bfloat16


def _const_spec(shape):
    nd = len(shape)
    return pl.BlockSpec(shape, lambda *_: (0,) * nd, pipeline_mode=pl.Buffered(1))


def _params(*sem):
    return pltpu.CompilerParams(dimension_semantics=sem, vmem_limit_bytes=VMEM_LIMIT)


def _rms(x, g):
    return x * lax.rsqrt(jnp.mean(x * x, axis=-1, keepdims=True) + EPS) * g


def _dot(a, b):
    return jnp.dot(a, b, preferred_element_type=F32)


class _SideCast(NamedTuple):
    src: Any
    in_block: tuple
    in_index: Callable[[Any], tuple]
    out_shape: tuple
    out_block: tuple
    out_index: Callable[[Any], tuple]
    transpose: bool


def _num_blocks(units, steps):
    return max(k for k in range(1, min(units, steps) + 1) if units % k == 0)


def _row_cast(w, steps):
    rows, cols = w.shape
    nb = _num_blocks(rows // BF16_SUBLANES, steps)
    rb = rows // nb
    idx = lambda s: (jnp.minimum(s, nb - 1), 0)
    return _SideCast(w, (rb, cols), idx, (rows, cols), (rb, cols), idx, False)


def _transposed_cast(wt, row0, n, steps):
    nb = _num_blocks(n // LANES, steps)
    cb = n // nb
    d = wt.shape[1]
    assert row0 % F32_SUBLANES == 0
    row = lambda s: pl.multiple_of(row0 + jnp.minimum(s, nb - 1) * cb, F32_SUBLANES)
    return _SideCast(wt, (pl.Element(cb), pl.Element(d)), lambda s: (row(s), 0),
                     (d, n), (d, cb), lambda s: (0, jnp.minimum(s, nb - 1)), True)


def _side_specs(jobs, step_of):
    ins = [pl.BlockSpec(j.in_block, lambda *g, j=j: j.in_index(step_of(*g))) for j in jobs]
    outs = [pl.BlockSpec(j.out_block, lambda *g, j=j: j.out_index(step_of(*g))) for j in jobs]
    shapes = [jax.ShapeDtypeStruct(j.out_shape, BF16) for j in jobs]
    return ins, outs, shapes


def _run_side_casts(in_refs, out_refs, transposes):
    for i_ref, o_ref, transpose in zip(in_refs, out_refs, transposes):
        v = i_ref[...]
        o_ref[...] = (v.T if transpose else v).astype(BF16)


def _ada_kernel(c_ref, w_ref, b_ref, *refs, side):
    ns = len(side)
    _run_side_casts(refs[:ns], refs[ns + 1:], side)
    c = c_ref[...]
    s = (c * jax.nn.sigmoid(c)).astype(BF16)
    refs[ns][...] = _dot(s, w_ref[...].astype(BF16)) + b_ref[...]


def _ada(cvec, w, b, side_jobs):
    d = cvec.shape[1]
    jobs = side_jobs(N_ADA)
    s_in, s_out, s_shape = _side_specs(jobs, lambda j: j)
    out, *side_out = pl.pallas_call(
        functools.partial(_ada_kernel, side=tuple(j.transpose for j in jobs)),
        grid=(N_ADA,),
        in_specs=[
            pl.BlockSpec((MOD_ROWS, d), lambda j: (0, 0)),
            pl.BlockSpec((d, d), lambda j: (0, j)),
            pl.BlockSpec((1, d), lambda j: (0, j)),
            *s_in,
        ],
        out_specs=[pl.BlockSpec((MOD_ROWS, d), lambda j: (0, j)), *s_out],
        out_shape=[jax.ShapeDtypeStruct((MOD_ROWS, N_ADA * d), F32), *s_shape],
        compiler_params=_params("arbitrary"),
        name="ada",
    )(cvec, w, b, *[j.src for j in jobs])
    return (out.reshape(MOD_ROWS, N_ADA, d), *side_out)


FF_CHUNK = 256
FFN_SUB = 512


def _ffn_kernel(x_ref, mod_ref, g_ref, w13_ref, w2_ref, *refs, mod0, g0, side):
    ns = len(side)
    o_ref, hid_ref = refs[ns], refs[2 * ns + 1]
    _run_side_casts(refs[:ns], refs[ns + 1:2 * ns + 1], side)
    shift = mod_ref[mod0:mod0 + 1, :]
    gain_pre = g_ref[g0:g0 + 1, :] * (1.0 + mod_ref[mod0 + 1:mod0 + 2, :])
    gain_post = 0.5 * mod_ref[mod0 + 2:mod0 + 3, :] * g_ref[g0 + 1:g0 + 2, :]
    nsub, sub, _ = hid_ref.shape
    rows = lambda s: slice(s * sub, (s + 1) * sub)

    for s in range(nsub):
        x = x_ref[rows(s), :]
        u = (_rms(x, gain_pre) + shift).astype(BF16)
        for c in range(FF_HALF // FF_CHUNK):
            lo = c * FF_CHUNK
            a = _dot(u, w13_ref[:, lo:lo + FF_CHUNK])
            b = _dot(u, w13_ref[:, FF_HALF + lo:FF_HALF + lo + FF_CHUNK])
            hid_ref[s, :, lo:lo + FF_CHUNK] = (a * jax.nn.sigmoid(a) * b).astype(BF16)
        y = _dot(hid_ref[s], w2_ref[...])
        o_ref[rows(s), :] = x + _rms(y, gain_post)


def _ffn(h, mod, g, w13, w2, *, tm, tiles_per_seq, mod_row0, mod_step, mod0, g0, name, side_jobs=None):
    n, d = h.shape
    jobs = side_jobs(n // tm) if side_jobs else ()
    s_in, s_out, s_shape = _side_specs(jobs, lambda i: i)
    kern = functools.partial(_ffn_kernel, mod0=mod0, g0=g0, side=tuple(j.transpose for j in jobs))
    sub = min(tm, FFN_SUB)
    outs = pl.pallas_call(
        kern,
        grid=(n // tm,),
        in_specs=[
            pl.BlockSpec((tm, d), lambda i: (i, 0)),
            pl.BlockSpec((None, N_ADA, d), lambda i: (mod_row0 + mod_step * (i // tiles_per_seq), 0, 0)),
            _const_spec(g.shape),
            _const_spec(w13.shape),
            _const_spec(w2.shape),
            *s_in,
        ],
        out_specs=[pl.BlockSpec((tm, d), lambda i: (i, 0)), *s_out],
        out_shape=[jax.ShapeDtypeStruct((n, d), F32), *s_shape],
        scratch_shapes=[pltpu.VMEM((tm // sub, sub, FF_HALF), BF16)],
        compiler_params=_params("arbitrary"),
        name=name,
    )(h, mod, g, w13, w2, *[j.src for j in jobs])
    return outs if jobs else outs[0]


QK_CHUNK = 256


def _w_in_jobs(wt, steps):
    return (_transposed_cast(wt, 0, COL_GATES, steps),
            _transposed_cast(wt, COL_GATES, LANES, steps),
            _transposed_cast(wt, COL_BR, wt.shape[0] - COL_BR, steps))


def _proj_kernel(x_ref, xp_ref, xn_ref, mod_ref, g_ref, w_ref, wg_ref, wbr_ref, b_ref, bg_ref, bbr_ref,
                 cw_ref, cb_ref, *refs, tm, tiles_per_seq, full):
    if full:
        xf_ref, q_ref, kt_ref, v_ref, o_ref, gt_ref, gf_ref, gm_ref, u_ref, p_ref = refs
    else:
        kt_ref, v_ref, gt_ref, u_ref, p_ref = refs
    i = pl.program_id(0)
    first = (i % tiles_per_seq) == 0
    last = (i % tiles_per_seq) == tiles_per_seq - 1
    shift = mod_ref[3:4, :]
    gain = g_ref[2:3, :] * (1.0 + mod_ref[4:5, :])

    def mk_u(x):
        return (_rms(x, gain) + shift).astype(BF16)

    u_ref[0:HALO, :] = mk_u(xp_ref[...])
    u_ref[HALO:HALO + tm, :] = mk_u(x_ref[...])
    u_ref[HALO + tm:, :] = mk_u(xn_ref[...])
    u = u_ref[HALO:HALO + tm, :]

    def lin(dst_ref, src_ref, bias_ref, lo, k):
        cols = slice(lo + k, lo + k + QK_CHUNK)
        val = (_dot(u, src_ref[:, cols]) + bias_ref[:, cols]).astype(dst_ref.dtype)
        if len(dst_ref.shape) == 3:
            dst_ref[k // QK_CHUNK] = val
        else:
            dst_ref[:, k:k + QK_CHUNK] = val

    def gates():
        gt_ref[...] = (_dot(u, wg_ref[...]) + bg_ref[...]).T[0:N_GATES, :]

    top = slice(HALO - F32_SUBLANES, HALO)
    bot = slice(HALO + tm, HALO + tm + F32_SUBLANES)

    def conv(c):
        lo = c * QK_CHUNK
        pb = p_ref.at[c % 2]
        p = _dot(u_ref[...], w_ref[:, COL_Q + lo:COL_Q + lo + QK_CHUNK]) + b_ref[:, COL_Q + lo:COL_Q + lo + QK_CHUNK]
        pb[...] = p
        pb[top, :] = jnp.where(first, 0.0, p[top, :])
        pb[bot, :] = jnp.where(last, 0.0, p[bot, :])
        cw = cw_ref[:, lo:lo + QK_CHUNK]
        y = (cw[0:1, :] * pb[HALO - 1:HALO - 1 + tm, :] + cw[1:2, :] * pb[HALO:HALO + tm, :]
             + cw[2:3, :] * pb[HALO + 1:HALO + 1 + tm, :]) + cb_ref[:, lo:lo + QK_CHUNK]
        y = y * jax.nn.sigmoid(y)
        if lo < M_WIDTH:
            q_ref[lo // QK_CHUNK] = y.astype(BF16)
        else:
            kt_ref[lo - M_WIDTH:lo - M_WIDTH + QK_CHUNK, :] = (y * (M_HEAD_DIM ** -0.5)).T.astype(BF16)

    chunks = lambda width: range(0, width, QK_CHUNK)
    plain = [functools.partial(lin, v_ref, w_ref, b_ref, COL_V, k) for k in chunks(M_WIDTH)] + [gates]
    if full:
        plain += [functools.partial(lin, xf_ref, w_ref, b_ref, 0, k) for k in chunks(F_WIDTH)]
        plain += [functools.partial(lin, o_ref, w_ref, b_ref, COL_O, k) for k in chunks(M_WIDTH)]
        plain += [functools.partial(lin, gf_ref, wbr_ref, bbr_ref, 0, k) for k in chunks(D_MODEL)]
        plain += [functools.partial(lin, gm_ref, wbr_ref, bbr_ref, D_MODEL, k) for k in chunks(D_MODEL)]
    convs = list(range(0 if full else M_WIDTH // QK_CHUNK, 2 * M_WIDTH // QK_CHUNK))
    per_conv = -(-len(plain) // len(convs))
    for n_done, c in enumerate(convs):
        conv(c)
        for task in plain[n_done * per_conv:(n_done + 1) * per_conv]:
            task()


def _proj(h, mod, g, ws, bs, cw, cb, *, tm, tiles_per_seq, mod_row0, mod_step, full, name):
    n, d = h.shape
    hb = tm // HALO
    nhb = n // HALO
    kern = functools.partial(_proj_kernel, tm=tm, tiles_per_seq=tiles_per_seq, full=full)
    tok = lambda width, dt: (pl.BlockSpec((tm, width), lambda i: (i, 0)), jax.ShapeDtypeStruct((n, width), dt))
    tr = lambda width, dt: (pl.BlockSpec((width, tm), lambda i: (0, i)), jax.ShapeDtypeStruct((width, n), dt))
    assert QK_CHUNK == M_HEAD_DIM
    heads = (pl.BlockSpec((M_HEADS, tm, M_HEAD_DIM), lambda i: (0, i, 0)),
             jax.ShapeDtypeStruct((M_HEADS, n, M_HEAD_DIM), BF16))
    xf, o, gf, gm = tok(F_WIDTH, BF16), tok(M_WIDTH, BF16), tok(d, BF16), tok(d, BF16)
    q = v = heads
    kt = tr(M_WIDTH, BF16)
    gt = tr(N_GATES, F32)
    outs = (xf, q, kt, v, o, gt, gf, gm) if full else (kt, v, gt)
    return pl.pallas_call(
        kern,
        grid=(n // tm,),
        in_specs=[
            pl.BlockSpec((tm, d), lambda i: (i, 0)),
            pl.BlockSpec((HALO, d), lambda i: (jnp.maximum(i * hb - 1, 0), 0)),
            pl.BlockSpec((HALO, d), lambda i: (jnp.minimum((i + 1) * hb, nhb - 1), 0)),
            pl.BlockSpec((None, N_ADA, d), lambda i: (mod_row0 + mod_step * (i // tiles_per_seq), 0, 0)),
            _const_spec(g.shape),
            *[_const_spec(a.shape) for a in (*ws, *bs, cw, cb)],
        ],
        out_specs=[s for s, _ in outs],
        out_shape=[s for _, s in outs],
        scratch_shapes=[pltpu.VMEM((tm + 2 * HALO, d), BF16),
                        pltpu.VMEM((2, tm + 2 * HALO, QK_CHUNK), F32)],
        compiler_params=_params("parallel"),
        name=name,
    )(h, h, h, mod, g, *ws, *bs, cw, cb)


def _lane_scan(x, reverse):
    lane = lax.broadcasted_iota(jnp.int32, x.shape, 1)
    k = 1
    while k < CHUNK:
        if reverse:
            x = x + jnp.where(lane < CHUNK - k, pltpu.roll(x, CHUNK - k, 1), 0.0)
        else:
            x = x + jnp.where(lane >= k, pltpu.roll(x, k, 1), 0.0)
        k *= 2
    return x


def _mlstm_kernel(q_ref, kt_ref, v_ref, ktc_ref, vc_ref, gl_ref, gc_ref, o_ref,
                  c_ref, n_ref, r_ref, wk_ref, lf_ref, ms_ref, dec_ref,
                  num_ref, den_ref, a_ref, e_ref, *, ncc, ncl):
    L = CHUNK
    dv = M_HEAD_DIM
    half = ncl // 2
    t_idx = lax.broadcasted_iota(jnp.int32, (L, L), 0)
    s_idx = lax.broadcasted_iota(jnp.int32, (L, L), 1)
    masks = (s_idx <= t_idx, s_idx >= t_idx)

    def gate_rows(g_ref, d):
        li = g_ref[2 * d] * LOG2E
        gf = g_ref[2 * d + 1]
        lf = (jnp.minimum(gf, 0.0) - jnp.log1p(jnp.exp(-jnp.abs(gf)))) * LOG2E
        bcum = _lane_scan(lf, reverse=(d == 1))
        r = li - bcum
        rmax = jnp.max(r, axis=-1, keepdims=True)
        b_end = bcum[:, L - 1:L] if d == 0 else bcum[:, 0:1]
        return lf, r, rmax, b_end

    for d in range(2):
        parts = (gate_rows(gl_ref, d), gate_rows(gc_ref, d))
        rev = d == 1
        order = [(1, c) for c in (range(ncc - 1, -1, -1) if rev else range(ncc))]
        order += [(0, c) for c in (range(ncl - 1, -1, -1) if rev else range(ncl))]
        m = jnp.zeros((1, 1), F32)
        for part, c in order:
            row = c + (ncl if part == 1 else 0)
            ms_ref[d, row:row + 1, :] = jnp.broadcast_to(m, (1, LANES))
            _, _, rmax, b_end = parts[part]
            m = b_end[c:c + 1, :] + jnp.maximum(m, rmax[c:c + 1, :])
        for part, (lo, cnt) in enumerate(((0, ncl), (ncl, ncc))):
            lf, r, rmax, _ = parts[part]
            m_start = ms_ref[d, lo:lo + cnt, 0:1]
            m_end = jnp.maximum(m_start, rmax)
            dec_ref[d, lo:lo + cnt, :] = jnp.broadcast_to(jnp.exp2(m_start - m_end), (cnt, LANES))
            wk_ref[d, lo:lo + cnt, :] = jnp.exp2(r - m_end)
            r_ref[d, lo:lo + cnt, :] = r
            lf_ref[d, lo:lo + cnt, :] = lf

    c_ref[...] = jnp.zeros_like(c_ref)
    n_ref[...] = jnp.zeros_like(n_ref)

    def update_state(d, row, kt_c, v_c):
        wk = wk_ref[d, pl.ds(row, 1), :]
        dec = dec_ref[d, pl.ds(row, 1), :]
        ktw = kt_c.astype(F32) * wk
        c_ref[d] = jnp.concatenate([dec] * (dv // LANES), axis=1) * c_ref[d] + _dot(ktw.astype(BF16), v_c)
        n_ref[d] = dec * n_ref[d] + jnp.sum(ktw, axis=-1, keepdims=True)

    def chunk_refs(c):
        off = pl.multiple_of(c * L, L)
        return off, q_ref[pl.ds(off, L), :], kt_ref[:, pl.ds(off, L)], v_ref[pl.ds(off, L), :]

    def intra(d, c, slot):
        _, q_c, kt_c, v_c = chunk_refs(c)
        rows = pl.ds(pl.multiple_of(slot * L, L), L)
        r_row = r_ref[d, pl.ds(c, 1), :]
        lf_row = lf_ref[d, pl.ds(c, 1), :]
        m = ms_ref[d, pl.ds(c, 1), :]
        rm = jnp.where(masks[d], r_row, -jnp.inf)
        mt = jnp.maximum(m, jnp.max(rm, axis=-1, keepdims=True))
        bc = jnp.sum(jnp.where(masks[d], lf_row, 0.0), axis=-1, keepdims=True)
        s = _dot(q_c, kt_c) * jnp.exp2(rm - mt)
        num_ref[d, rows, :] = _dot(s.astype(BF16), v_c)
        den_ref[d, rows, :] = jnp.broadcast_to(jnp.sum(s, axis=-1, keepdims=True), (L, LANES))
        a_ref[d, rows, :] = jnp.exp2(m - mt)
        e_ref[d, rows, :] = jnp.exp2(-bc - mt)

    def inter(d, c, slot, accumulate):
        off, q_c, kt_c, v_c = chunk_refs(c)
        rows = pl.ds(pl.multiple_of(slot * L, L), L)
        a = a_ref[d, rows, :]
        state = jnp.concatenate([c_ref[d].astype(BF16), n_ref[d].astype(BF16)], axis=1)
        qcn = _dot(q_c, state)
        den = a * qcn[:, dv:] + den_ref[d, rows, :]
        inv = 1.0 / jnp.maximum(jnp.abs(den), e_ref[d, rows, :])
        for k in range(0, dv, LANES):
            h = (a * qcn[:, k:k + LANES] + num_ref[d, rows, k:k + LANES]) * inv
            if accumulate:
                o_ref[pl.ds(off, L), k:k + LANES] += h
            else:
                o_ref[pl.ds(off, L), k:k + LANES] = h
        update_state(d, c, kt_c, v_c)

    for d in range(2):
        for c in (range(ncc) if d == 0 else range(ncc - 1, -1, -1)):
            update_state(d, ncl + c, ktc_ref[:, c * L:(c + 1) * L], vc_ref[c * L:(c + 1) * L, :])

    def run_half(first, accumulate):
        def chunk_of(d, g):
            return first + g if d == 0 else ncl - 1 - first - g

        def intra_step(g, carry):
            for d in range(2):
                intra(d, chunk_of(d, g), g)
            return carry

        def inter_step(g, carry):
            for d in range(2):
                inter(d, chunk_of(d, g), g, accumulate)
            return carry

        lax.fori_loop(0, half, intra_step, 0, unroll=4)
        lax.fori_loop(0, half, inter_step, 0, unroll=4)

    run_half(0, False)
    run_half(half, True)


def _mlstm(q, kt, v, ktc, vc, gl, gc, *, batch, t_lat, t_ctx):
    ncl = t_lat // CHUNK
    ncc = t_ctx // CHUNK
    nc = ncc + ncl
    assert ncl % 4 == 0
    slot_rows = t_lat // 2
    dh = M_HEAD_DIM
    kern = functools.partial(_mlstm_kernel, ncc=ncc, ncl=ncl)
    return pl.pallas_call(
        kern,
        grid=(batch, M_HEADS),
        in_specs=[
            pl.BlockSpec((None, t_lat, dh), lambda b, h: (h, b, 0)),
            pl.BlockSpec((dh, t_lat), lambda b, h: (h, b)),
            pl.BlockSpec((None, t_lat, dh), lambda b, h: (h, b, 0)),
            pl.BlockSpec((dh, t_ctx), lambda b, h: (h, b)),
            pl.BlockSpec((None, t_ctx, dh), lambda b, h: (h, b, 0)),
            pl.BlockSpec((4, None, None, ncl, CHUNK), lambda b, h: (0, h, b, 0, 0)),
            pl.BlockSpec((4, None, None, ncc, CHUNK), lambda b, h: (0, h, b, 0, 0)),
        ],
        out_specs=pl.BlockSpec((None, t_lat, dh), lambda b, h: (h, b, 0)),
        out_shape=jax.ShapeDtypeStruct((M_HEADS, batch * t_lat, dh), F32),
        scratch_shapes=[
            pltpu.VMEM((2, dh, dh), F32),
            pltpu.VMEM((2, dh, LANES), F32),
            pltpu.VMEM((2, nc, CHUNK), F32),
            pltpu.VMEM((2, nc, CHUNK), F32),
            pltpu.VMEM((2, nc, CHUNK), F32),
            pltpu.VMEM((2, nc, LANES), F32),
            pltpu.VMEM((2, nc, LANES), F32),
            pltpu.VMEM((2, slot_rows, dh), F32),
            pltpu.VMEM((2, slot_rows, LANES), F32),
            pltpu.VMEM((2, slot_rows, LANES), F32),
            pltpu.VMEM((2, slot_rows, LANES), F32),
        ],
        compiler_params=_params("parallel", "arbitrary"),
        name="mlstm",
    )(q, kt, v, ktc, vc, gl, gc)


ROW_BLOCK = 4
ROW_SUB = 8
GROUPS_PER_STEP = 2


def _fourier_kernel(x_ref, bd_ref, w2_ref, kc_ref, ks_ref, *refs, rows, scale, side):
    ns = len(side)
    o_ref, zr_ref, zi_ref = refs[ns], refs[2 * ns + 1], refs[2 * ns + 2]
    _run_side_casts(refs[:ns], refs[ns + 1:2 * ns + 1], side)
    gd = F_GROUP_DIM
    nb = ROW_BLOCK * GRID_W
    for blk in range(rows // ROW_BLOCK):
        pq = _dot(bd_ref[...], x_ref[blk * nb:(blk + 1) * nb, :])
        ar = pq[0:nb, :].astype(BF16)
        ai = pq[nb:, :].astype(BF16)
        rs = slice(blk * ROW_BLOCK, (blk + 1) * ROW_BLOCK)
        for g in range(GROUPS_PER_STEP):
            cs = slice(g * gd, (g + 1) * gd)
            z = _dot(jnp.concatenate([ar[:, cs], ai[:, cs]], axis=1), w2_ref[...])
            zr_ref[rs, :, cs] = z[:, 0:gd].reshape(ROW_BLOCK, GRID_W, gd)
            zi_ref[rs, :, cs] = z[:, gd:].reshape(ROW_BLOCK, GRID_W, gd)
    nw = GROUPS_PER_STEP * gd
    for j in range(GRID_W // ROW_SUB):
        js = slice(j * ROW_SUB, (j + 1) * ROW_SUB)
        zr = zr_ref[:, js, :].reshape(rows * ROW_SUB, nw).astype(BF16)
        zi = zi_ref[:, js, :].reshape(rows * ROW_SUB, nw).astype(BF16)
        y = _dot(kc_ref[...], zr) + _dot(ks_ref[...], zi)
        o_ref[:, js, :] = (y * scale).reshape(rows, ROW_SUB, nw)


def _dft_mats(n):
    k = np.arange(n)
    ang = 2.0 * np.pi * np.outer(k, k) / n
    return np.cos(ang), np.sin(ang)


def _fourier(xf, *, batch, t_lat, side_jobs):
    rows = t_lat // GRID_W
    assert rows % ROW_BLOCK == 0
    ck, sk = _dft_mats(F_GROUP_DIM)
    cc, sc = _dft_mats(GRID_W)
    cr, sr = _dft_mats(rows)
    eye_b = np.eye(ROW_BLOCK)
    eye_s = np.eye(ROW_SUB)
    tables = (
        np.concatenate([np.kron(eye_b, cc), np.kron(eye_b, -sc)], axis=0),
        np.block([[ck, -sk], [sk, ck]]),
        np.kron(cr, eye_s), np.kron(sr, eye_s),
    )
    bd, w2, kc, ks = (jnp.asarray(t, F32).astype(BF16) for t in tables)
    scale = 1.0 / math.sqrt(rows * GRID_W * F_GROUP_DIM)
    gsteps = F_GROUPS // GROUPS_PER_STEP
    jobs = side_jobs(batch * gsteps)
    s_in, s_out, s_shape = _side_specs(jobs, lambda b, g: b * gsteps + g)
    kern = functools.partial(_fourier_kernel, rows=rows, scale=scale, side=tuple(j.transpose for j in jobs))
    nw = GROUPS_PER_STEP * F_GROUP_DIM
    out, *side_out = pl.pallas_call(
        kern,
        grid=(batch, gsteps),
        in_specs=[
            pl.BlockSpec((t_lat, nw), lambda b, g: (b, g)),
            _const_spec(bd.shape), _const_spec(w2.shape), _const_spec(kc.shape), _const_spec(ks.shape),
            *s_in,
        ],
        out_specs=[pl.BlockSpec((None, rows, GRID_W, nw), lambda b, g: (b, 0, 0, g)), *s_out],
        out_shape=[jax.ShapeDtypeStruct((batch, rows, GRID_W, F_WIDTH), F32), *s_shape],
        scratch_shapes=[
            pltpu.VMEM((rows, GRID_W, nw), F32),
            pltpu.VMEM((rows, GRID_W, nw), F32),
        ],
        compiler_params=_params("arbitrary", "arbitrary"),
        name="fourier",
    )(xf, bd, w2, kc, ks, *[j.src for j in jobs])
    return (out.reshape(batch * t_lat, F_WIDTH), *side_out)


def _merge_kernel(x_ref, yf_ref, h_ref, o_ref, gf_ref, gm_ref, mod_ref, g_ref, hg_ref,
                  wf_ref, wm_ref, wo_ref, out_ref, hm_ref):
    dh = M_HEAD_DIM
    for hd in range(M_HEADS):
        sl = slice(hd * dh, (hd + 1) * dh)
        hn = _rms(h_ref[hd], hg_ref[:, sl])
        hm_ref[:, sl] = (jax.nn.sigmoid(o_ref[:, sl].astype(F32)) * hn).astype(BF16)
    y = (jax.nn.sigmoid(gf_ref[...].astype(F32)) * _dot(yf_ref[...].astype(BF16), wf_ref[...])
         + jax.nn.sigmoid(gm_ref[...].astype(F32)) * _dot(hm_ref[...], wm_ref[...]))
    out = _dot(y.astype(BF16), wo_ref[...])
    out_ref[...] = x_ref[...] + _rms(out, mod_ref[5:6, :] * g_ref[3:4, :])


def _merge(x, yf, h, o, gf, gm, mod, g, hg, wf, wm, wo, *, tm, tiles_per_seq):
    n, d = x.shape
    tok = lambda width: pl.BlockSpec((tm, width), lambda i: (i, 0))
    return pl.pallas_call(
        _merge_kernel,
        grid=(n // tm,),
        in_specs=[
            tok(d), tok(F_WIDTH),
            pl.BlockSpec((M_HEADS, tm, M_HEAD_DIM), lambda i: (0, i, 0)),
            tok(M_WIDTH), tok(d), tok(d),
            pl.BlockSpec((None, N_ADA, d), lambda i: (i // tiles_per_seq, 0, 0)),
            _const_spec(g.shape), _const_spec(hg.shape),
            _const_spec(wf.shape), _const_spec(wm.shape), _const_spec(wo.shape),
        ],
        out_specs=tok(d),
        out_shape=jax.ShapeDtypeStruct((n, d), F32),
        scratch_shapes=[pltpu.VMEM((tm, M_WIDTH), BF16)],
        compiler_params=_params("parallel"),
        name="merge",
    )(x, yf, h, o, gf, gm, mod, g, hg, wf, wm, wo)


def _tile(t, cap):
    tm = min(t, cap)
    assert t % tm == 0
    return tm


def kernel(x, c, ctx, c_ctx, w_ada, b_ada, norm_g, w13_a, w2_a, w_in, b_in, conv_w, conv_b,
           head_g, w_four, w_mproj, w_out, w13_b, w2_b):
    batch, t_lat, d = x.shape
    t_ctx = ctx.shape[1]
    assert d == D_MODEL and w_ada.shape[0] == 1, "single-layer kernel"
    assert batch + 1 <= MOD_ROWS and t_lat % (GRID_W * CHUNK // math.gcd(GRID_W, CHUNK)) == 0
    assert t_ctx % CHUNK == 0

    cvec = jnp.concatenate([c, c_ctx[None], jnp.zeros((MOD_ROWS - batch - 1, d), F32)], axis=0)
    first_jobs = lambda steps: (_row_cast(w13_a[0], steps), _row_cast(w2_a[0], steps))
    mod, w13a, w2a = _ada(cvec, w_ada[0], b_ada[0][None], first_jobs)
    g = norm_g[0]

    gpad = LANES - N_GATES
    bi = b_in[0][None]
    b_p = (bi[:, :COL_GATES], jnp.pad(bi[:, COL_GATES:COL_BR], ((0, 0), (0, gpad))), bi[:, COL_BR:])
    cw, cb, hg = conv_w[0], conv_b[0][None], head_g[0][None]
    w_in_jobs = functools.partial(_w_in_jobs, w_in[0].T)
    late_jobs = lambda steps: tuple(_row_cast(w[0], steps) for w in (w13_b, w2_b, w_four, w_mproj, w_out))

    tm_l = _tile(t_lat, 512)
    tm_c = _tile(t_ctx, 512)
    tps_l = t_lat // tm_l
    tps_c = t_ctx // tm_c
    xl = x.reshape(batch * t_lat, d)
    xc = ctx.reshape(batch * t_ctx, d)

    lat = dict(tm=tm_l, tiles_per_seq=tps_l, mod_row0=0, mod_step=1)
    con = dict(tm=tm_c, tiles_per_seq=tps_c, mod_row0=batch, mod_step=0)
    tm_f = _tile(t_lat, 2 * FFN_SUB)
    lat_ffn = dict(lat, tm=tm_f, tiles_per_seq=t_lat // tm_f)
    con_ffn = dict(con, tm=_tile(batch * t_ctx, 2 * FFN_SUB))

    hl, *w_p = _ffn(xl, mod, g, w13a, w2a, mod0=0, g0=0, name="ffn_a_lat", side_jobs=w_in_jobs, **lat_ffn)
    hc = _ffn(xc, mod, g, w13a, w2a, mod0=0, g0=0, name="ffn_a_ctx", **con_ffn)

    xf, q, kt, v, o, gt, gf, gm = _proj(hl, mod, g, w_p, b_p, cw, cb, full=True, name="proj_lat", **lat)
    ktc, vc, gtc = _proj(hc, mod, g, w_p, b_p, cw, cb, full=False, name="proj_ctx", **con)

    gl = gt.reshape(4, M_HEADS, batch, t_lat // CHUNK, CHUNK)
    gc = gtc.reshape(4, M_HEADS, batch, t_ctx // CHUNK, CHUNK)
    hm = _mlstm(q, kt, v, ktc, vc, gl, gc, batch=batch, t_lat=t_lat, t_ctx=t_ctx)
    yf, w13b, w2b, wf, wm, wo = _fourier(xf, batch=batch, t_lat=t_lat, side_jobs=late_jobs)

    hl = _merge(hl, yf, hm, o, gf, gm, mod, g, hg, wf, wm, wo, tm=tm_l, tiles_per_seq=tps_l)
    hl = _ffn(hl, mod, g, w13b, w2b, mod0=6, g0=4, name="ffn_b_lat", **lat_ffn)
    return hl.reshape(batch, t_lat, d)
```

```python
import functools
import math
from typing import Any, Callable, NamedTuple

import numpy as np
import jax
import jax.numpy as jnp
from jax import lax
from jax.experimental import pallas as pl
from jax.experimental.pallas import tpu as pltpu

D_MODEL = 1024
GRID_W = 64
FF_HALF = 2816
F_GROUPS = 4
F_GROUP_DIM = 128
F_WIDTH = F_GROUPS * F_GROUP_DIM
M_HEADS = 4
M_HEAD_DIM = 256
M_WIDTH = M_HEADS * M_HEAD_DIM
CONV_K = 3
CHUNK = 128
N_ADA = 9
EPS = 1e-6
LOG2E = math.log2(math.e)

COL_Q = F_WIDTH
COL_K = COL_Q + M_WIDTH
COL_V = COL_K + M_WIDTH
COL_O = COL_V + M_WIDTH
COL_GATES = COL_O + M_WIDTH
N_GATES = 4 * M_HEADS
COL_BR = COL_GATES + N_GATES

LANES = 128
F32_SUBLANES = 8
BF16_SUBLANES = 16
MOD_ROWS = 8
HALO = BF16_SUBLANES
VMEM_LIMIT = 56 * 1024 * 1024

F32 = jnp.float32
BF16 = jnp.bfloat16


def _const_spec(shape):
    nd = len(shape)
    return pl.BlockSpec(shape, lambda *_: (0,) * nd, pipeline_mode=pl.Buffered(1))


def _params(*sem):
    return pltpu.CompilerParams(dimension_semantics=sem, vmem_limit_bytes=VMEM_LIMIT)


def _rms(x, g):
    return x * lax.rsqrt(jnp.mean(x * x, axis=-1, keepdims=True) + EPS) * g


def _dot(a, b):
    return jnp.dot(a, b, preferred_element_type=F32)


class _SideCast(NamedTuple):
    src: Any
    in_block: tuple
    in_index: Callable[[Any], tuple]
    out_shape: tuple
    out_block: tuple
    out_index: Callable[[Any], tuple]
    transpose: bool


def _num_blocks(units, steps):
    return max(k for k in range(1, min(units, steps) + 1) if units % k == 0)


def _row_cast(w, steps):
    rows, cols = w.shape
    nb = _num_blocks(rows // BF16_SUBLANES, steps)
    rb = rows // nb
    idx = lambda s: (jnp.minimum(s, nb - 1), 0)
    return _SideCast(w, (rb, cols), idx, (rows, cols), (rb, cols), idx, False)


def _transposed_cast(wt, row0, n, steps):
    nb = _num_blocks(n // LANES, steps)
    cb = n // nb
    d = wt.shape[1]
    assert row0 % F32_SUBLANES == 0
    row = lambda s: pl.multiple_of(row0 + jnp.minimum(s, nb - 1) * cb, F32_SUBLANES)
    return _SideCast(wt, (pl.Element(cb), pl.Element(d)), lambda s: (row(s), 0),
                     (d, n), (d, cb), lambda s: (0, jnp.minimum(s, nb - 1)), True)


def _side_specs(jobs, step_of):
    ins = [pl.BlockSpec(j.in_block, lambda *g, j=j: j.in_index(step_of(*g))) for j in jobs]
    outs = [pl.BlockSpec(j.out_block, lambda *g, j=j: j.out_index(step_of(*g))) for j in jobs]
    shapes = [jax.ShapeDtypeStruct(j.out_shape, BF16) for j in jobs]
    return ins, outs, shapes


def _run_side_casts(in_refs, out_refs, transposes):
    for i_ref, o_ref, transpose in zip(in_refs, out_refs, transposes):
        v = i_ref[...]
        o_ref[...] = (v.T if transpose else v).astype(BF16)


def _ada_kernel(c_ref, w_ref, b_ref, *refs, side):
    ns = len(side)
    _run_side_casts(refs[:ns], refs[ns + 1:], side)
    c = c_ref[...]
    s = (c * jax.nn.sigmoid(c)).astype(BF16)
    refs[ns][...] = _dot(s, w_ref[...].astype(BF16)) + b_ref[...]


def _ada(cvec, w, b, side_jobs):
    d = cvec.shape[1]
    jobs = side_jobs(N_ADA)
    s_in, s_out, s_shape = _side_specs(jobs, lambda j: j)
    out, *side_out = pl.pallas_call(
        functools.partial(_ada_kernel, side=tuple(j.transpose for j in jobs)),
        grid=(N_ADA,),
        in_specs=[
            pl.BlockSpec((MOD_ROWS, d), lambda j: (0, 0)),
            pl.BlockSpec((d, d), lambda j: (0, j)),
            pl.BlockSpec((1, d), lambda j: (0, j)),
            *s_in,
        ],
        out_specs=[pl.BlockSpec((MOD_ROWS, d), lambda j: (0, j)), *s_out],
        out_shape=[jax.ShapeDtypeStruct((MOD_ROWS, N_ADA * d), F32), *s_shape],
        compiler_params=_params("arbitrary"),
        name="ada",
    )(cvec, w, b, *[j.src for j in jobs])
    return (out.reshape(MOD_ROWS, N_ADA, d), *side_out)


FF_CHUNK = 256
FFN_SUB = 512


def _ffn_kernel(x_ref, mod_ref, g_ref, w13_ref, w2_ref, *refs, mod0, g0, side):
    ns = len(side)
    o_ref, hid_ref = refs[ns], refs[2 * ns + 1]
    _run_side_casts(refs[:ns], refs[ns + 1:2 * ns + 1], side)
    shift = mod_ref[mod0:mod0 + 1, :]
    gain_pre = g_ref[g0:g0 + 1, :] * (1.0 + mod_ref[mod0 + 1:mod0 + 2, :])
    gain_post = 0.5 * mod_ref[mod0 + 2:mod0 + 3, :] * g_ref[g0 + 1:g0 + 2, :]
    nsub, sub, _ = hid_ref.shape
    rows = lambda s: slice(s * sub, (s + 1) * sub)

    for s in range(nsub):
        o_ref[rows(s), :] = _ffn_rows(x_ref[rows(s), :], shift, gain_pre, gain_post, w13_ref, w2_ref, hid_ref.at[s])


def _ffn_rows(x, shift, gain_pre, gain_post, w13_ref, w2_ref, hid_ref):
    u = (_rms(x, gain_pre) + shift).astype(BF16)
    for c in range(FF_HALF // FF_CHUNK):
        lo = c * FF_CHUNK
        a = _dot(u, w13_ref[:, lo:lo + FF_CHUNK])
        b = _dot(u, w13_ref[:, FF_HALF + lo:FF_HALF + lo + FF_CHUNK])
        hid_ref[:, lo:lo + FF_CHUNK] = (a * jax.nn.sigmoid(a) * b).astype(BF16)
    return x + _rms(_dot(hid_ref[...], w2_ref[...]), gain_post)


def _ffn(h, mod, g, w13, w2, *, tm, tiles_per_seq, mod_row0, mod_step, mod0, g0, name, side_jobs=None):
    n, d = h.shape
    jobs = side_jobs(n // tm) if side_jobs else ()
    s_in, s_out, s_shape = _side_specs(jobs, lambda i: i)
    kern = functools.partial(_ffn_kernel, mod0=mod0, g0=g0, side=tuple(j.transpose for j in jobs))
    sub = min(tm, FFN_SUB)
    outs = pl.pallas_call(
        kern,
        grid=(n // tm,),
        in_specs=[
            pl.BlockSpec((tm, d), lambda i: (i, 0)),
            pl.BlockSpec((None, N_ADA, d), lambda i: (mod_row0 + mod_step * (i // tiles_per_seq), 0, 0)),
            _const_spec(g.shape),
            _const_spec(w13.shape),
            _const_spec(w2.shape),
            *s_in,
        ],
        out_specs=[pl.BlockSpec((tm, d), lambda i: (i, 0)), *s_out],
        out_shape=[jax.ShapeDtypeStruct((n, d), F32), *s_shape],
        scratch_shapes=[pltpu.VMEM((tm // sub, sub, FF_HALF), BF16)],
        compiler_params=_params("arbitrary"),
        name=name,
    )(h, mod, g, w13, w2, *[j.src for j in jobs])
    return outs if jobs else outs[0]


QK_CHUNK = 256


def _w_in_jobs(wt, steps):
    return (_transposed_cast(wt, 0, COL_GATES, steps),
            _transposed_cast(wt, COL_GATES, LANES, steps),
            _transposed_cast(wt, COL_BR, wt.shape[0] - COL_BR, steps))


def _proj_kernel(x_ref, xp_ref, xn_ref, mod_ref, g_ref, w_ref, wg_ref, wbr_ref, b_ref, bg_ref, bbr_ref,
                 cw_ref, cb_ref, *refs, tm, tiles_per_seq, full):
    if full:
        xf_ref, q_ref, kt_ref, v_ref, o_ref, gt_ref, gf_ref, gm_ref, u_ref, p_ref = refs
    else:
        kt_ref, v_ref, gt_ref, u_ref, p_ref = refs
    i = pl.program_id(0)
    first = (i % tiles_per_seq) == 0
    last = (i % tiles_per_seq) == tiles_per_seq - 1
    shift = mod_ref[3:4, :]
    gain = g_ref[2:3, :] * (1.0 + mod_ref[4:5, :])

    def mk_u(x):
        return (_rms(x, gain) + shift).astype(BF16)

    u_ref[0:HALO, :] = mk_u(xp_ref[...])
    u_ref[HALO:HALO + tm, :] = mk_u(x_ref[...])
    u_ref[HALO + tm:, :] = mk_u(xn_ref[...])
    u = u_ref[HALO:HALO + tm, :]

    def lin(dst_ref, src_ref, bias_ref, lo, k):
        cols = slice(lo + k, lo + k + QK_CHUNK)
        val = (_dot(u, src_ref[:, cols]) + bias_ref[:, cols]).astype(dst_ref.dtype)
        if len(dst_ref.shape) == 3:
            dst_ref[k // QK_CHUNK] = val
        else:
            dst_ref[:, k:k + QK_CHUNK] = val

    def gates():
        gt_ref[...] = (_dot(u, wg_ref[...]) + bg_ref[...]).T[0:N_GATES, :]

    top = slice(HALO - F32_SUBLANES, HALO)
    bot = slice(HALO + tm, HALO + tm + F32_SUBLANES)

    def conv(c):
        lo = c * QK_CHUNK
        pb = p_ref.at[c % 2]
        p = _dot(u_ref[...], w_ref[:, COL_Q + lo:COL_Q + lo + QK_CHUNK]) + b_ref[:, COL_Q + lo:COL_Q + lo + QK_CHUNK]
        pb[...] = p
        pb[top, :] = jnp.where(first, 0.0, p[top, :])
        pb[bot, :] = jnp.where(last, 0.0, p[bot, :])
        cw = cw_ref[:, lo:lo + QK_CHUNK]
        y = (cw[0:1, :] * pb[HALO - 1:HALO - 1 + tm, :] + cw[1:2, :] * pb[HALO:HALO + tm, :]
             + cw[2:3, :] * pb[HALO + 1:HALO + 1 + tm, :]) + cb_ref[:, lo:lo + QK_CHUNK]
        y = y * jax.nn.sigmoid(y)
        if lo < M_WIDTH:
            q_ref[lo // QK_CHUNK] = y.astype(BF16)
        else:
            kt_ref[lo - M_WIDTH:lo - M_WIDTH + QK_CHUNK, :] = (y * (M_HEAD_DIM ** -0.5)).T.astype(BF16)

    chunks = lambda width: range(0, width, QK_CHUNK)
    plain = [functools.partial(lin, v_ref, w_ref, b_ref, COL_V, k) for k in chunks(M_WIDTH)] + [gates]
    if full:
        plain += [functools.partial(lin, xf_ref, w_ref, b_ref, 0, k) for k in chunks(F_WIDTH)]
        plain += [functools.partial(lin, o_ref, w_ref, b_ref, COL_O, k) for k in chunks(M_WIDTH)]
        plain += [functools.partial(lin, gf_ref, wbr_ref, bbr_ref, 0, k) for k in chunks(D_MODEL)]
        plain += [functools.partial(lin, gm_ref, wbr_ref, bbr_ref, D_MODEL, k) for k in chunks(D_MODEL)]
    convs = list(range(0 if full else M_WIDTH // QK_CHUNK, 2 * M_WIDTH // QK_CHUNK))
    per_conv = -(-len(plain) // len(convs))
    for n_done, c in enumerate(convs):
        conv(c)
        for task in plain[n_done * per_conv:(n_done + 1) * per_conv]:
            task()


def _proj(h, mod, g, ws, bs, cw, cb, *, tm, tiles_per_seq, mod_row0, mod_step, full, name):
    n, d = h.shape
    hb = tm // HALO
    nhb = n // HALO
    kern = functools.partial(_proj_kernel, tm=tm, tiles_per_seq=tiles_per_seq, full=full)
    tok = lambda width, dt: (pl.BlockSpec((tm, width), lambda i: (i, 0)), jax.ShapeDtypeStruct((n, width), dt))
    tr = lambda width, dt: (pl.BlockSpec((width, tm), lambda i: (0, i)), jax.ShapeDtypeStruct((width, n), dt))
    assert QK_CHUNK == M_HEAD_DIM
    heads = (pl.BlockSpec((M_HEADS, tm, M_HEAD_DIM), lambda i: (0, i, 0)),
             jax.ShapeDtypeStruct((M_HEADS, n, M_HEAD_DIM), BF16))
    xf, o, gf, gm = tok(F_WIDTH, BF16), tok(M_WIDTH, BF16), tok(d, BF16), tok(d, BF16)
    q = v = heads
    kt = tr(M_WIDTH, BF16)
    gt = tr(N_GATES, F32)
    outs = (xf, q, kt, v, o, gt, gf, gm) if full else (kt, v, gt)
    return pl.pallas_call(
        kern,
        grid=(n // tm,),
        in_specs=[
            pl.BlockSpec((tm, d), lambda i: (i, 0)),
            pl.BlockSpec((HALO, d), lambda i: (jnp.maximum(i * hb - 1, 0), 0)),
            pl.BlockSpec((HALO, d), lambda i: (jnp.minimum((i + 1) * hb, nhb - 1), 0)),
            pl.BlockSpec((None, N_ADA, d), lambda i: (mod_row0 + mod_step * (i // tiles_per_seq), 0, 0)),
            _const_spec(g.shape),
            *[_const_spec(a.shape) for a in (*ws, *bs, cw, cb)],
        ],
        out_specs=[s for s, _ in outs],
        out_shape=[s for _, s in outs],
        scratch_shapes=[pltpu.VMEM((tm + 2 * HALO, d), BF16),
                        pltpu.VMEM((2, tm + 2 * HALO, QK_CHUNK), F32)],
        compiler_params=_params("parallel"),
        name=name,
    )(h, h, h, mod, g, *ws, *bs, cw, cb)


def _lane_scan(x, reverse):
    lane = lax.broadcasted_iota(jnp.int32, x.shape, 1)
    k = 1
    while k < CHUNK:
        if reverse:
            x = x + jnp.where(lane < CHUNK - k, pltpu.roll(x, CHUNK - k, 1), 0.0)
        else:
            x = x + jnp.where(lane >= k, pltpu.roll(x, k, 1), 0.0)
        k *= 2
    return x


def _mlstm_kernel(q_ref, kt_ref, v_ref, ktc_ref, vc_ref, gl_ref, gc_ref, o_ref,
                  c_ref, n_ref, r_ref, wk_ref, lf_ref, ms_ref, dec_ref,
                  num_ref, den_ref, a_ref, e_ref, *, ncc, ncl):
    L = CHUNK
    dv = M_HEAD_DIM
    half = ncl // 2
    t_idx = lax.broadcasted_iota(jnp.int32, (L, L), 0)
    s_idx = lax.broadcasted_iota(jnp.int32, (L, L), 1)
    masks = (s_idx <= t_idx, s_idx >= t_idx)

    def gate_rows(g_ref, d):
        li = g_ref[2 * d] * LOG2E
        gf = g_ref[2 * d + 1]
        lf = (jnp.minimum(gf, 0.0) - jnp.log1p(jnp.exp(-jnp.abs(gf)))) * LOG2E
        bcum = _lane_scan(lf, reverse=(d == 1))
        r = li - bcum
        rmax = jnp.max(r, axis=-1, keepdims=True)
        b_end = bcum[:, L - 1:L] if d == 0 else bcum[:, 0:1]
        return lf, r, rmax, b_end

    for d in range(2):
        parts = (gate_rows(gl_ref, d), gate_rows(gc_ref, d))
        rev = d == 1
        order = [(1, c) for c in (range(ncc - 1, -1, -1) if rev else range(ncc))]
        order += [(0, c) for c in (range(ncl - 1, -1, -1) if rev else range(ncl))]
        m = jnp.zeros((1, 1), F32)
        for part, c in order:
            row = c + (ncl if part == 1 else 0)
            ms_ref[d, row:row + 1, :] = jnp.broadcast_to(m, (1, LANES))
            _, _, rmax, b_end = parts[part]
            m = b_end[c:c + 1, :] + jnp.maximum(m, rmax[c:c + 1, :])
        for part, (lo, cnt) in enumerate(((0, ncl), (ncl, ncc))):
            lf, r, rmax, _ = parts[part]
            m_start = ms_ref[d, lo:lo + cnt, 0:1]
            m_end = jnp.maximum(m_start, rmax)
            dec_ref[d, lo:lo + cnt, :] = jnp.broadcast_to(jnp.exp2(m_start - m_end), (cnt, LANES))
            wk_ref[d, lo:lo + cnt, :] = jnp.exp2(r - m_end)
            r_ref[d, lo:lo + cnt, :] = r
            lf_ref[d, lo:lo + cnt, :] = lf

    c_ref[...] = jnp.zeros_like(c_ref)
    n_ref[...] = jnp.zeros_like(n_ref)

    def update_state(d, row, kt_c, v_c):
        wk = wk_ref[d, pl.ds(row, 1), :]
        dec = dec_ref[d, pl.ds(row, 1), :]
        ktw = kt_c.astype(F32) * wk
        c_ref[d] = jnp.concatenate([dec] * (dv // LANES), axis=1) * c_ref[d] + _dot(ktw.astype(BF16), v_c)
        n_ref[d] = dec * n_ref[d] + jnp.sum(ktw, axis=-1, keepdims=True)

    def chunk_refs(c):
        off = pl.multiple_of(c * L, L)
        return off, q_ref[pl.ds(off, L), :], kt_ref[:, pl.ds(off, L)], v_ref[pl.ds(off, L), :]

    def intra(d, c, slot):
        _, q_c, kt_c, v_c = chunk_refs(c)
        rows = pl.ds(pl.multiple_of(slot * L, L), L)
        r_row = r_ref[d, pl.ds(c, 1), :]
        lf_row = lf_ref[d, pl.ds(c, 1), :]
        m = ms_ref[d, pl.ds(c, 1), :]
        rm = jnp.where(masks[d], r_row, -jnp.inf)
        mt = jnp.maximum(m, jnp.max(rm, axis=-1, keepdims=True))
        bc = jnp.sum(jnp.where(masks[d], lf_row, 0.0), axis=-1, keepdims=True)
        s = _dot(q_c, kt_c) * jnp.exp2(rm - mt)
        num_ref[d, rows, :] = _dot(s.astype(BF16), v_c)
        den_ref[d, rows, :] = jnp.broadcast_to(jnp.sum(s, axis=-1, keepdims=True), (L, LANES))
        a_ref[d, rows, :] = jnp.exp2(m - mt)
        e_ref[d, rows, :] = jnp.exp2(-bc - mt)

    def inter(d, c, slot, accumulate):
        off, q_c, kt_c, v_c = chunk_refs(c)
        rows = pl.ds(pl.multiple_of(slot * L, L), L)
        a = a_ref[d, rows, :]
        state = jnp.concatenate([c_ref[d].astype(BF16), n_ref[d].astype(BF16)], axis=1)
        qcn = _dot(q_c, state)
        den = a * qcn[:, dv:] + den_ref[d, rows, :]
        inv = 1.0 / jnp.maximum(jnp.abs(den), e_ref[d, rows, :])
        for k in range(0, dv, LANES):
            h = (a * qcn[:, k:k + LANES] + num_ref[d, rows, k:k + LANES]) * inv
            if accumulate:
                o_ref[pl.ds(off, L), k:k + LANES] += h
            else:
                o_ref[pl.ds(off, L), k:k + LANES] = h
        update_state(d, c, kt_c, v_c)

    for d in range(2):
        for c in (range(ncc) if d == 0 else range(ncc - 1, -1, -1)):
            update_state(d, ncl + c, ktc_ref[:, c * L:(c + 1) * L], vc_ref[c * L:(c + 1) * L, :])

    def run_half(first, accumulate):
        def chunk_of(d, g):
            return first + g if d == 0 else ncl - 1 - first - g

        def intra_step(g, carry):
            for d in range(2):
                intra(d, chunk_of(d, g), g)
            return carry

        def inter_step(g, carry):
            for d in range(2):
                inter(d, chunk_of(d, g), g, accumulate)
            return carry

        lax.fori_loop(0, half, intra_step, 0, unroll=8)
        lax.fori_loop(0, half, inter_step, 0, unroll=4)

    run_half(0, False)
    run_half(half, True)


def _mlstm(q, kt, v, ktc, vc, gl, gc, *, batch, t_lat, t_ctx):
    ncl = t_lat // CHUNK
    ncc = t_ctx // CHUNK
    nc = ncc + ncl
    assert ncl % 4 == 0
    slot_rows = t_lat // 2
    dh = M_HEAD_DIM
    kern = functools.partial(_mlstm_kernel, ncc=ncc, ncl=ncl)
    return pl.pallas_call(
        kern,
        grid=(batch, M_HEADS),
        in_specs=[
            pl.BlockSpec((None, t_lat, dh), lambda b, h: (h, b, 0)),
            pl.BlockSpec((dh, t_lat), lambda b, h: (h, b)),
            pl.BlockSpec((None, t_lat, dh), lambda b, h: (h, b, 0)),
            pl.BlockSpec((dh, t_ctx), lambda b, h: (h, b)),
            pl.BlockSpec((None, t_ctx, dh), lambda b, h: (h, b, 0)),
            pl.BlockSpec((4, None, None, ncl, CHUNK), lambda b, h: (0, h, b, 0, 0)),
            pl.BlockSpec((4, None, None, ncc, CHUNK), lambda b, h: (0, h, b, 0, 0)),
        ],
        out_specs=pl.BlockSpec((None, t_lat, dh), lambda b, h: (h, b, 0)),
        out_shape=jax.ShapeDtypeStruct((M_HEADS, batch * t_lat, dh), F32),
        scratch_shapes=[
            pltpu.VMEM((2, dh, dh), F32),
            pltpu.VMEM((2, dh, LANES), F32),
            pltpu.VMEM((2, nc, CHUNK), F32),
            pltpu.VMEM((2, nc, CHUNK), F32),
            pltpu.VMEM((2, nc, CHUNK), F32),
            pltpu.VMEM((2, nc, LANES), F32),
            pltpu.VMEM((2, nc, LANES), F32),
            pltpu.VMEM((2, slot_rows, dh), F32),
            pltpu.VMEM((2, slot_rows, LANES), F32),
            pltpu.VMEM((2, slot_rows, LANES), F32),
            pltpu.VMEM((2, slot_rows, LANES), F32),
        ],
        compiler_params=_params("parallel", "arbitrary"),
        name="mlstm",
    )(q, kt, v, ktc, vc, gl, gc)


ROW_BLOCK = 4
ROW_SUB = 8
GROUPS_PER_STEP = 2


def _fourier_kernel(x_ref, bd_ref, w2_ref, kc_ref, ks_ref, *refs, rows, scale, side):
    ns = len(side)
    o_ref, zr_ref, zi_ref = refs[ns], refs[2 * ns + 1], refs[2 * ns + 2]
    _run_side_casts(refs[:ns], refs[ns + 1:2 * ns + 1], side)
    gd = F_GROUP_DIM
    nb = ROW_BLOCK * GRID_W
    for blk in range(rows // ROW_BLOCK):
        pq = _dot(bd_ref[...], x_ref[blk * nb:(blk + 1) * nb, :])
        ar = pq[0:nb, :].astype(BF16)
        ai = pq[nb:, :].astype(BF16)
        rs = slice(blk * ROW_BLOCK, (blk + 1) * ROW_BLOCK)
        for g in range(GROUPS_PER_STEP):
            cs = slice(g * gd, (g + 1) * gd)
            z = _dot(jnp.concatenate([ar[:, cs], ai[:, cs]], axis=1), w2_ref[...])
            zr_ref[rs, :, cs] = z[:, 0:gd].reshape(ROW_BLOCK, GRID_W, gd)
            zi_ref[rs, :, cs] = z[:, gd:].reshape(ROW_BLOCK, GRID_W, gd)
    nw = GROUPS_PER_STEP * gd
    for j in range(GRID_W // ROW_SUB):
        js = slice(j * ROW_SUB, (j + 1) * ROW_SUB)
        zr = zr_ref[:, js, :].reshape(rows * ROW_SUB, nw).astype(BF16)
        zi = zi_ref[:, js, :].reshape(rows * ROW_SUB, nw).astype(BF16)
        y = _dot(kc_ref[...], zr) + _dot(ks_ref[...], zi)
        o_ref[:, js, :] = (y * scale).reshape(rows, ROW_SUB, nw)


def _dft_mats(n):
    k = np.arange(n)
    ang = 2.0 * np.pi * np.outer(k, k) / n
    return np.cos(ang), np.sin(ang)


def _fourier(xf, *, batch, t_lat, side_jobs):
    rows = t_lat // GRID_W
    assert rows % ROW_BLOCK == 0
    ck, sk = _dft_mats(F_GROUP_DIM)
    cc, sc = _dft_mats(GRID_W)
    cr, sr = _dft_mats(rows)
    eye_b = np.eye(ROW_BLOCK)
    eye_s = np.eye(ROW_SUB)
    tables = (
        np.concatenate([np.kron(eye_b, cc), np.kron(eye_b, -sc)], axis=0),
        np.block([[ck, -sk], [sk, ck]]),
        np.kron(cr, eye_s), np.kron(sr, eye_s),
    )
    bd, w2, kc, ks = (jnp.asarray(t, F32).astype(BF16) for t in tables)
    scale = 1.0 / math.sqrt(rows * GRID_W * F_GROUP_DIM)
    gsteps = F_GROUPS // GROUPS_PER_STEP
    jobs = side_jobs(batch * gsteps)
    s_in, s_out, s_shape = _side_specs(jobs, lambda b, g: b * gsteps + g)
    kern = functools.partial(_fourier_kernel, rows=rows, scale=scale, side=tuple(j.transpose for j in jobs))
    nw = GROUPS_PER_STEP * F_GROUP_DIM
    out, *side_out = pl.pallas_call(
        kern,
        grid=(batch, gsteps),
        in_specs=[
            pl.BlockSpec((t_lat, nw), lambda b, g: (b, g)),
            _const_spec(bd.shape), _const_spec(w2.shape), _const_spec(kc.shape), _const_spec(ks.shape),
            *s_in,
        ],
        out_specs=[pl.BlockSpec((None, rows, GRID_W, nw), lambda b, g: (b, 0, 0, g)), *s_out],
        out_shape=[jax.ShapeDtypeStruct((batch, rows, GRID_W, F_WIDTH), F32), *s_shape],
        scratch_shapes=[
            pltpu.VMEM((rows, GRID_W, nw), F32),
            pltpu.VMEM((rows, GRID_W, nw), F32),
        ],
        compiler_params=_params("arbitrary", "arbitrary"),
        name="fourier",
    )(xf, bd, w2, kc, ks, *[j.src for j in jobs])
    return (out.reshape(batch * t_lat, F_WIDTH), *side_out)


def _merge_ffn_kernel(x_ref, yf_ref, h_ref, o_ref, gf_ref, gm_ref, mod_ref, g_ref, hg_ref,
                      wf_ref, wm_ref, wo_ref, w13_ref, w2_ref, out_ref, hm_ref, hid_ref):
    dh = M_HEAD_DIM
    for hd in range(M_HEADS):
        sl = slice(hd * dh, (hd + 1) * dh)
        hn = _rms(h_ref[hd], hg_ref[:, sl])
        hm_ref[:, sl] = (jax.nn.sigmoid(o_ref[:, sl].astype(F32)) * hn).astype(BF16)
    y = (jax.nn.sigmoid(gf_ref[...].astype(F32)) * _dot(yf_ref[...].astype(BF16), wf_ref[...])
         + jax.nn.sigmoid(gm_ref[...].astype(F32)) * _dot(hm_ref[...], wm_ref[...]))
    out = _dot(y.astype(BF16), wo_ref[...])
    x1 = x_ref[...] + _rms(out, mod_ref[5:6, :] * g_ref[3:4, :])
    gain_pre = g_ref[4:5, :] * (1.0 + mod_ref[7:8, :])
    gain_post = 0.5 * mod_ref[8:9, :] * g_ref[5:6, :]
    out_ref[...] = _ffn_rows(x1, mod_ref[6:7, :], gain_pre, gain_post, w13_ref, w2_ref, hid_ref)


def _merge_ffn(x, yf, h, o, gf, gm, mod, g, hg, wf, wm, wo, w13, w2, *, tm, tiles_per_seq):
    n, d = x.shape
    tok = lambda width: pl.BlockSpec((tm, width), lambda i: (i, 0))
    return pl.pallas_call(
        _merge_ffn_kernel,
        grid=(n // tm,),
        in_specs=[
            tok(d), tok(F_WIDTH),
            pl.BlockSpec((M_HEADS, tm, M_HEAD_DIM), lambda i: (0, i, 0)),
            tok(M_WIDTH), tok(d), tok(d),
            pl.BlockSpec((None, N_ADA, d), lambda i: (i // tiles_per_seq, 0, 0)),
            *[_const_spec(a.shape) for a in (g, hg, wf, wm, wo, w13, w2)],
        ],
        out_specs=tok(d),
        out_shape=jax.ShapeDtypeStruct((n, d), F32),
        scratch_shapes=[pltpu.VMEM((tm, M_WIDTH), BF16), pltpu.VMEM((tm, FF_HALF), BF16)],
        compiler_params=_params("parallel"),
        name="merge_ffn_b",
    )(x, yf, h, o, gf, gm, mod, g, hg, wf, wm, wo, w13, w2)


def _tile(t, cap):
    tm = min(t, cap)
    assert t % tm == 0
    return tm


def kernel(x, c, ctx, c_ctx, w_ada, b_ada, norm_g, w13_a, w2_a, w_in, b_in, conv_w, conv_b,
           head_g, w_four, w_mproj, w_out, w13_b, w2_b):
    batch, t_lat, d = x.shape
    t_ctx = ctx.shape[1]
    assert d == D_MODEL and w_ada.shape[0] == 1, "single-layer kernel"
    assert batch + 1 <= MOD_ROWS and t_lat % (GRID_W * CHUNK // math.gcd(GRID_W, CHUNK)) == 0
    assert t_ctx % CHUNK == 0

    cvec = jnp.concatenate([c, c_ctx[None], jnp.zeros((MOD_ROWS - batch - 1, d), F32)], axis=0)
    first_jobs = lambda steps: (_row_cast(w13_a[0], steps), _row_cast(w2_a[0], steps))
    mod, w13a, w2a = _ada(cvec, w_ada[0], b_ada[0][None], first_jobs)
    g = norm_g[0]

    gpad = LANES - N_GATES
    bi = b_in[0][None]
    b_p = (bi[:, :COL_GATES], jnp.pad(bi[:, COL_GATES:COL_BR], ((0, 0), (0, gpad))), bi[:, COL_BR:])
    cw, cb, hg = conv_w[0], conv_b[0][None], head_g[0][None]
    w_in_jobs = functools.partial(_w_in_jobs, w_in[0].T)
    late_jobs = lambda steps: tuple(_row_cast(w[0], steps) for w in (w13_b, w2_b, w_four, w_mproj, w_out))

    tm_l = _tile(t_lat, 512)
    tm_c = _tile(t_ctx, 512)
    tps_l = t_lat // tm_l
    tps_c = t_ctx // tm_c
    xl = x.reshape(batch * t_lat, d)
    xc = ctx.reshape(batch * t_ctx, d)

    lat = dict(tm=tm_l, tiles_per_seq=tps_l, mod_row0=0, mod_step=1)
    con = dict(tm=tm_c, tiles_per_seq=tps_c, mod_row0=batch, mod_step=0)
    tm_f = _tile(t_lat, 2 * FFN_SUB)
    lat_ffn = dict(lat, tm=tm_f, tiles_per_seq=t_lat // tm_f)
    con_ffn = dict(con, tm=_tile(batch * t_ctx, 2 * FFN_SUB))

    hl, *w_p = _ffn(xl, mod, g, w13a, w2a, mod0=0, g0=0, name="ffn_a_lat", side_jobs=w_in_jobs, **lat_ffn)
    hc = _ffn(xc, mod, g, w13a, w2a, mod0=0, g0=0, name="ffn_a_ctx", **con_ffn)

    xf, q, kt, v, o, gt, gf, gm = _proj(hl, mod, g, w_p, b_p, cw, cb, full=True, name="proj_lat", **lat)
    ktc, vc, gtc = _proj(hc, mod, g, w_p, b_p, cw, cb, full=False, name="proj_ctx", **con)

    gl = gt.reshape(4, M_HEADS, batch, t_lat // CHUNK, CHUNK)
    gc = gtc.reshape(4, M_HEADS, batch, t_ctx // CHUNK, CHUNK)
    hm = _mlstm(q, kt, v, ktc, vc, gl, gc, batch=batch, t_lat=t_lat, t_ctx=t_ctx)
    yf, w13b, w2b, wf, wm, wo = _fourier(xf, batch=batch, t_lat=t_lat, side_jobs=late_jobs)

    hl = _merge_ffn(hl, yf, hm, o, gf, gm, mod, g, hg, wf, wm, wo, w13b, w2b, tm=tm_l, tiles_per_seq=tps_l)
    return hl.reshape(batch, t_lat, d)
```

```python
import functools
import math
from typing import Any, Callable, NamedTuple

import numpy as np
import jax
import jax.numpy as jnp
from jax import lax
from jax.experimental import pallas as pl
from jax.experimental.pallas import tpu as pltpu

D_MODEL = 1024
GRID_W = 64
FF_HALF = 2816
F_GROUPS = 4
F_GROUP_DIM = 128
F_WIDTH = F_GROUPS * F_GROUP_DIM
M_HEADS = 4
M_HEAD_DIM = 256
M_WIDTH = M_HEADS * M_HEAD_DIM
CONV_K = 3
CHUNK = 128
N_ADA = 9
EPS = 1e-6
LOG2E = math.log2(math.e)

COL_Q = F_WIDTH
COL_K = COL_Q + M_WIDTH
COL_V = COL_K + M_WIDTH
COL_O = COL_V + M_WIDTH
COL_GATES = COL_O + M_WIDTH
N_GATES = 4 * M_HEADS
COL_BR = COL_GATES + N_GATES

LANES = 128
F32_SUBLANES = 8
BF16_SUBLANES = 16
MOD_ROWS = 8
HALO = BF16_SUBLANES
VMEM_LIMIT = 56 * 1024 * 1024

F32 = jnp.float32
BF16 = jnp.bfloat16


def _const_spec(shape):
    nd = len(shape)
    return pl.BlockSpec(shape, lambda *_: (0,) * nd, pipeline_mode=pl.Buffered(1))


def _params(*sem):
    return pltpu.CompilerParams(dimension_semantics=sem, vmem_limit_bytes=VMEM_LIMIT)


def _rms(x, g):
    return x * lax.rsqrt(jnp.mean(x * x, axis=-1, keepdims=True) + EPS) * g


def _dot(a, b):
    return jnp.dot(a, b, preferred_element_type=F32)


class _SideCast(NamedTuple):
    src: Any
    in_block: tuple
    in_index: Callable[[Any], tuple]
    out_shape: tuple
    out_block: tuple
    out_index: Callable[[Any], tuple]
    transpose: bool


def _num_blocks(units, steps):
    return max(k for k in range(1, min(units, steps) + 1) if units % k == 0)


def _row_cast(w, steps):
    rows, cols = w.shape
    nb = _num_blocks(rows // BF16_SUBLANES, steps)
    rb = rows // nb
    idx = lambda s: (jnp.minimum(s, nb - 1), 0)
    return _SideCast(w, (rb, cols), idx, (rows, cols), (rb, cols), idx, False)


def _transposed_cast(wt, row0, n, steps):
    nb = _num_blocks(n // LANES, steps)
    cb = n // nb
    d = wt.shape[1]
    assert row0 % F32_SUBLANES == 0
    row = lambda s: pl.multiple_of(row0 + jnp.minimum(s, nb - 1) * cb, F32_SUBLANES)
    return _SideCast(wt, (pl.Element(cb), pl.Element(d)), lambda s: (row(s), 0),
                     (d, n), (d, cb), lambda s: (0, jnp.minimum(s, nb - 1)), True)


def _side_specs(jobs, step_of):
    ins = [pl.BlockSpec(j.in_block, lambda *g, j=j: j.in_index(step_of(*g))) for j in jobs]
    outs = [pl.BlockSpec(j.out_block, lambda *g, j=j: j.out_index(step_of(*g))) for j in jobs]
    shapes = [jax.ShapeDtypeStruct(j.out_shape, BF16) for j in jobs]
    return ins, outs, shapes


def _run_side_casts(in_refs, out_refs, transposes):
    for i_ref, o_ref, transpose in zip(in_refs, out_refs, transposes):
        v = i_ref[...]
        o_ref[...] = (v.T if transpose else v).astype(BF16)


def _ada_kernel(c_ref, w_ref, b_ref, *refs, side):
    ns = len(side)
    _run_side_casts(refs[:ns], refs[ns + 1:], side)
    c = c_ref[...]
    s = (c * jax.nn.sigmoid(c)).astype(BF16)
    refs[ns][...] = _dot(s, w_ref[...].astype(BF16)) + b_ref[...]


def _ada(cvec, w, b, side_jobs):
    d = cvec.shape[1]
    jobs = side_jobs(N_ADA)
    s_in, s_out, s_shape = _side_specs(jobs, lambda j: j)
    out, *side_out = pl.pallas_call(
        functools.partial(_ada_kernel, side=tuple(j.transpose for j in jobs)),
        grid=(N_ADA,),
        in_specs=[
            pl.BlockSpec((MOD_ROWS, d), lambda j: (0, 0)),
            pl.BlockSpec((d, d), lambda j: (0, j)),
            pl.BlockSpec((1, d), lambda j: (0, j)),
            *s_in,
        ],
        out_specs=[pl.BlockSpec((MOD_ROWS, d), lambda j: (0, j)), *s_out],
        out_shape=[jax.ShapeDtypeStruct((MOD_ROWS, N_ADA * d), F32), *s_shape],
        compiler_params=_params("arbitrary"),
        name="ada",
    )(cvec, w, b, *[j.src for j in jobs])
    return (out.reshape(MOD_ROWS, N_ADA, d), *side_out)


FF_CHUNK = 256
FFN_SUB = 512


def _ffn_kernel(x_ref, mod_ref, g_ref, w13_ref, w2_ref, *refs, mod0, g0, side):
    ns = len(side)
    o_ref, hid_ref = refs[ns], refs[2 * ns + 1]
    _run_side_casts(refs[:ns], refs[ns + 1:2 * ns + 1], side)
    shift = mod_ref[mod0:mod0 + 1, :]
    gain_pre = g_ref[g0:g0 + 1, :] * (1.0 + mod_ref[mod0 + 1:mod0 + 2, :])
    gain_post = 0.5 * mod_ref[mod0 + 2:mod0 + 3, :] * g_ref[g0 + 1:g0 + 2, :]
    nsub, sub, _ = hid_ref.shape
    rows = lambda s: slice(s * sub, (s + 1) * sub)

    for s in range(nsub):
        o_ref[rows(s), :] = _ffn_rows(x_ref[rows(s), :], shift, gain_pre, gain_post, w13_ref, w2_ref, hid_ref.at[s])


def _ffn_rows(x, shift, gain_pre, gain_post, w13_ref, w2_ref, hid_ref):
    u = (_rms(x, gain_pre) + shift).astype(BF16)
    for c in range(FF_HALF // FF_CHUNK):
        lo = c * FF_CHUNK
        a = _dot(u, w13_ref[:, lo:lo + FF_CHUNK])
        b = _dot(u, w13_ref[:, FF_HALF + lo:FF_HALF + lo + FF_CHUNK])
        hid_ref[:, lo:lo + FF_CHUNK] = (a * jax.nn.sigmoid(a) * b).astype(BF16)
    return x + _rms(_dot(hid_ref[...], w2_ref[...]), gain_post)


def _ffn(h, mod, g, w13, w2, *, tm, tiles_per_seq, mod_row0, mod_step, mod0, g0, name, side_jobs=None):
    n, d = h.shape
    jobs = side_jobs(n // tm) if side_jobs else ()
    s_in, s_out, s_shape = _side_specs(jobs, lambda i: i)
    kern = functools.partial(_ffn_kernel, mod0=mod0, g0=g0, side=tuple(j.transpose for j in jobs))
    sub = min(tm, FFN_SUB)
    outs = pl.pallas_call(
        kern,
        grid=(n // tm,),
        in_specs=[
            pl.BlockSpec((tm, d), lambda i: (i, 0)),
            pl.BlockSpec((None, N_ADA, d), lambda i: (mod_row0 + mod_step * (i // tiles_per_seq), 0, 0)),
            _const_spec(g.shape),
            _const_spec(w13.shape),
            _const_spec(w2.shape),
            *s_in,
        ],
        out_specs=[pl.BlockSpec((tm, d), lambda i: (i, 0)), *s_out],
        out_shape=[jax.ShapeDtypeStruct((n, d), F32), *s_shape],
        scratch_shapes=[pltpu.VMEM((tm // sub, sub, FF_HALF), BF16)],
        compiler_params=_params("arbitrary"),
        name=name,
    )(h, mod, g, w13, w2, *[j.src for j in jobs])
    return outs if jobs else outs[0]


QK_CHUNK = 256


def _w_in_jobs(wt, steps):
    return (_transposed_cast(wt, 0, COL_GATES, steps),
            _transposed_cast(wt, COL_GATES, LANES, steps),
            _transposed_cast(wt, COL_BR, wt.shape[0] - COL_BR, steps))


def _proj_kernel(x_ref, xp_ref, xn_ref, mod_ref, g_ref, w_ref, wg_ref, wbr_ref, b_ref, bg_ref, bbr_ref,
                 cw_ref, cb_ref, *refs, tm, tiles_per_seq, full):
    if full:
        xf_ref, q_ref, kt_ref, v_ref, o_ref, gt_ref, gf_ref, gm_ref, u_ref, p_ref = refs
    else:
        kt_ref, v_ref, gt_ref, u_ref, p_ref = refs
    i = pl.program_id(0)
    first = (i % tiles_per_seq) == 0
    last = (i % tiles_per_seq) == tiles_per_seq - 1
    shift = mod_ref[3:4, :]
    gain = g_ref[2:3, :] * (1.0 + mod_ref[4:5, :])

    def mk_u(x):
        return (_rms(x, gain) + shift).astype(BF16)

    u_ref[0:HALO, :] = mk_u(xp_ref[...])
    u_ref[HALO:HALO + tm, :] = mk_u(x_ref[...])
    u_ref[HALO + tm:, :] = mk_u(xn_ref[...])
    u = u_ref[HALO:HALO + tm, :]

    def lin(dst_ref, src_ref, bias_ref, lo, k):
        cols = slice(lo + k, lo + k + QK_CHUNK)
        val = (_dot(u, src_ref[:, cols]) + bias_ref[:, cols]).astype(dst_ref.dtype)
        if len(dst_ref.shape) == 3:
            dst_ref[k // QK_CHUNK] = val
        else:
            dst_ref[:, k:k + QK_CHUNK] = val

    def gates():
        gt_ref[...] = (_dot(u, wg_ref[...]) + bg_ref[...]).T[0:N_GATES, :]

    top = slice(HALO - F32_SUBLANES, HALO)
    bot = slice(HALO + tm, HALO + tm + F32_SUBLANES)

    def conv(c):
        lo = c * QK_CHUNK
        pb = p_ref.at[c % 2]
        p = _dot(u_ref[...], w_ref[:, COL_Q + lo:COL_Q + lo + QK_CHUNK]) + b_ref[:, COL_Q + lo:COL_Q + lo + QK_CHUNK]
        pb[...] = p
        pb[top, :] = jnp.where(first, 0.0, p[top, :])
        pb[bot, :] = jnp.where(last, 0.0, p[bot, :])
        cw = cw_ref[:, lo:lo + QK_CHUNK]
        y = (cw[0:1, :] * pb[HALO - 1:HALO - 1 + tm, :] + cw[1:2, :] * pb[HALO:HALO + tm, :]
             + cw[2:3, :] * pb[HALO + 1:HALO + 1 + tm, :]) + cb_ref[:, lo:lo + QK_CHUNK]
        y = y * jax.nn.sigmoid(y)
        if lo < M_WIDTH:
            q_ref[lo // QK_CHUNK] = y.astype(BF16)
        else:
            kt_ref[lo - M_WIDTH:lo - M_WIDTH + QK_CHUNK, :] = (y * (M_HEAD_DIM ** -0.5)).T.astype(BF16)

    chunks = lambda width: range(0, width, QK_CHUNK)
    plain = [functools.partial(lin, v_ref, w_ref, b_ref, COL_V, k) for k in chunks(M_WIDTH)] + [gates]
    if full:
        plain += [functools.partial(lin, xf_ref, w_ref, b_ref, 0, k) for k in chunks(F_WIDTH)]
        plain += [functools.partial(lin, o_ref, w_ref, b_ref, COL_O, k) for k in chunks(M_WIDTH)]
        plain += [functools.partial(lin, gf_ref, wbr_ref, bbr_ref, 0, k) for k in chunks(D_MODEL)]
        plain += [functools.partial(lin, gm_ref, wbr_ref, bbr_ref, D_MODEL, k) for k in chunks(D_MODEL)]
    convs = list(range(0 if full else M_WIDTH // QK_CHUNK, 2 * M_WIDTH // QK_CHUNK))
    per_conv = -(-len(plain) // len(convs))
    for n_done, c in enumerate(convs):
        conv(c)
        for task in plain[n_done * per_conv:(n_done + 1) * per_conv]:
            task()


def _proj(h, mod, g, ws, bs, cw, cb, *, tm, tiles_per_seq, mod_row0, mod_step, full, name):
    n, d = h.shape
    hb = tm // HALO
    nhb = n // HALO
    kern = functools.partial(_proj_kernel, tm=tm, tiles_per_seq=tiles_per_seq, full=full)
    tok = lambda width, dt: (pl.BlockSpec((tm, width), lambda i: (i, 0)), jax.ShapeDtypeStruct((n, width), dt))
    tr = lambda width, dt: (pl.BlockSpec((width, tm), lambda i: (0, i)), jax.ShapeDtypeStruct((width, n), dt))
    assert QK_CHUNK == M_HEAD_DIM
    heads = (pl.BlockSpec((M_HEADS, tm, M_HEAD_DIM), lambda i: (0, i, 0)),
             jax.ShapeDtypeStruct((M_HEADS, n, M_HEAD_DIM), BF16))
    xf, o, gf, gm = tok(F_WIDTH, BF16), tok(M_WIDTH, BF16), tok(d, BF16), tok(d, BF16)
    q = v = heads
    kt = tr(M_WIDTH, BF16)
    gt = tr(N_GATES, F32)
    outs = (xf, q, kt, v, o, gt, gf, gm) if full else (kt, v, gt)
    return pl.pallas_call(
        kern,
        grid=(n // tm,),
        in_specs=[
            pl.BlockSpec((tm, d), lambda i: (i, 0)),
            pl.BlockSpec((HALO, d), lambda i: (jnp.maximum(i * hb - 1, 0), 0)),
            pl.BlockSpec((HALO, d), lambda i: (jnp.minimum((i + 1) * hb, nhb - 1), 0)),
            pl.BlockSpec((None, N_ADA, d), lambda i: (mod_row0 + mod_step * (i // tiles_per_seq), 0, 0)),
            _const_spec(g.shape),
            *[_const_spec(a.shape) for a in (*ws, *bs, cw, cb)],
        ],
        out_specs=[s for s, _ in outs],
        out_shape=[s for _, s in outs],
        scratch_shapes=[pltpu.VMEM((tm + 2 * HALO, d), BF16),
                        pltpu.VMEM((2, tm + 2 * HALO, QK_CHUNK), F32)],
        compiler_params=_params("parallel"),
        name=name,
    )(h, h, h, mod, g, *ws, *bs, cw, cb)


def _lane_scan(x, reverse):
    lane = lax.broadcasted_iota(jnp.int32, x.shape, 1)
    k = 1
    while k < CHUNK:
        if reverse:
            x = x + jnp.where(lane < CHUNK - k, pltpu.roll(x, CHUNK - k, 1), 0.0)
        else:
            x = x + jnp.where(lane >= k, pltpu.roll(x, k, 1), 0.0)
        k *= 2
    return x


def _mlstm_kernel(q_ref, kt_ref, v_ref, ktc_ref, vc_ref, gl_ref, gc_ref, o_ref,
                  c_ref, n_ref, r_all, wk_all, lf_all, ms_all, dec_all,
                  num_ref, den_ref, a_ref, e_ref, *, ncc, ncl):
    L = CHUNK
    dv = M_HEAD_DIM
    half = ncl // 2
    t_idx = lax.broadcasted_iota(jnp.int32, (L, L), 0)
    s_idx = lax.broadcasted_iota(jnp.int32, (L, L), 1)
    masks = (s_idx <= t_idx, s_idx >= t_idx)

    def gate_rows(g_ref, hd, d):
        li = g_ref[2 * d, hd] * LOG2E
        gf = g_ref[2 * d + 1, hd]
        lf = (jnp.minimum(gf, 0.0) - jnp.log1p(jnp.exp(-jnp.abs(gf)))) * LOG2E
        bcum = _lane_scan(lf, reverse=(d == 1))
        r = li - bcum
        rmax = jnp.max(r, axis=-1, keepdims=True)
        b_end = bcum[:, L - 1:L] if d == 0 else bcum[:, 0:1]
        return lf, r, rmax, b_end

    def gate_prologue(hd, d):
        parts = (gate_rows(gl_ref, hd, d), gate_rows(gc_ref, hd, d))
        rev = d == 1
        order = [(1, c) for c in (range(ncc - 1, -1, -1) if rev else range(ncc))]
        order += [(0, c) for c in (range(ncl - 1, -1, -1) if rev else range(ncl))]
        m = jnp.zeros((1, 1), F32)
        for part, c in order:
            row = c + (ncl if part == 1 else 0)
            ms_all[hd, d, row:row + 1, :] = jnp.broadcast_to(m, (1, LANES))
            _, _, rmax, b_end = parts[part]
            m = b_end[c:c + 1, :] + jnp.maximum(m, rmax[c:c + 1, :])
        for part, (lo, cnt) in enumerate(((0, ncl), (ncl, ncc))):
            lf, r, rmax, _ = parts[part]
            m_start = ms_all[hd, d, lo:lo + cnt, 0:1]
            m_end = jnp.maximum(m_start, rmax)
            dec_all[hd, d, lo:lo + cnt, :] = jnp.broadcast_to(jnp.exp2(m_start - m_end), (cnt, LANES))
            wk_all[hd, d, lo:lo + cnt, :] = jnp.exp2(r - m_end)
            r_all[hd, d, lo:lo + cnt, :] = r
            lf_all[hd, d, lo:lo + cnt, :] = lf

    head = pl.program_id(1)

    @pl.when(head == 0)
    def _():
        for hd in range(M_HEADS):
            for d in range(2):
                gate_prologue(hd, d)

    r_ref, wk_ref, lf_ref, ms_ref, dec_ref = (a.at[head] for a in (r_all, wk_all, lf_all, ms_all, dec_all))

    c_ref[...] = jnp.zeros_like(c_ref)
    n_ref[...] = jnp.zeros_like(n_ref)

    def update_state(d, row, kt_c, v_c):
        wk = wk_ref[d, pl.ds(row, 1), :]
        dec = dec_ref[d, pl.ds(row, 1), :]
        ktw = kt_c.astype(F32) * wk
        c_ref[d] = jnp.concatenate([dec] * (dv // LANES), axis=1) * c_ref[d] + _dot(ktw.astype(BF16), v_c)
        n_ref[d] = dec * n_ref[d] + jnp.sum(ktw, axis=-1, keepdims=True)

    def chunk_refs(c):
        off = pl.multiple_of(c * L, L)
        return off, q_ref[pl.ds(off, L), :], kt_ref[:, pl.ds(off, L)], v_ref[pl.ds(off, L), :]

    def intra(d, c, slot):
        _, q_c, kt_c, v_c = chunk_refs(c)
        rows = pl.ds(pl.multiple_of(slot * L, L), L)
        r_row = r_ref[d, pl.ds(c, 1), :]
        lf_row = lf_ref[d, pl.ds(c, 1), :]
        m = ms_ref[d, pl.ds(c, 1), :]
        rm = jnp.where(masks[d], r_row, -jnp.inf)
        mt = jnp.maximum(m, jnp.max(rm, axis=-1, keepdims=True))
        bc = jnp.sum(jnp.where(masks[d], lf_row, 0.0), axis=-1, keepdims=True)
        s = _dot(q_c, kt_c) * jnp.exp2(rm - mt)
        num_ref[d, rows, :] = _dot(s.astype(BF16), v_c)
        den_ref[d, rows, :] = jnp.broadcast_to(jnp.sum(s, axis=-1, keepdims=True), (L, LANES))
        a_ref[d, rows, :] = jnp.exp2(m - mt)
        e_ref[d, rows, :] = jnp.exp2(-bc - mt)

    def inter(d, c, slot, accumulate):
        off, q_c, kt_c, v_c = chunk_refs(c)
        rows = pl.ds(pl.multiple_of(slot * L, L), L)
        a = a_ref[d, rows, :]
        state = jnp.concatenate([c_ref[d].astype(BF16), n_ref[d].astype(BF16)], axis=1)
        qcn = _dot(q_c, state)
        den = a * qcn[:, dv:] + den_ref[d, rows, :]
        inv = 1.0 / jnp.maximum(jnp.abs(den), e_ref[d, rows, :])
        for k in range(0, dv, LANES):
            h = (a * qcn[:, k:k + LANES] + num_ref[d, rows, k:k + LANES]) * inv
            if accumulate:
                o_ref[pl.ds(off, L), k:k + LANES] += h
            else:
                o_ref[pl.ds(off, L), k:k + LANES] = h
        update_state(d, c, kt_c, v_c)

    for d in range(2):
        for c in (range(ncc) if d == 0 else range(ncc - 1, -1, -1)):
            update_state(d, ncl + c, ktc_ref[:, c * L:(c + 1) * L], vc_ref[c * L:(c + 1) * L, :])

    def run_half(first, accumulate):
        def chunk_of(d, g):
            return first + g if d == 0 else ncl - 1 - first - g

        def intra_step(g, carry):
            for d in range(2):
                intra(d, chunk_of(d, g), g)
            return carry

        def inter_step(g, carry):
            for d in range(2):
                inter(d, chunk_of(d, g), g, accumulate)
            return carry

        lax.fori_loop(0, half, intra_step, 0, unroll=8)
        lax.fori_loop(0, half, inter_step, 0, unroll=4)

    run_half(0, False)
    run_half(half, True)


def _mlstm(q, kt, v, ktc, vc, gl, gc, *, batch, t_lat, t_ctx):
    ncl = t_lat // CHUNK
    ncc = t_ctx // CHUNK
    nc = ncc + ncl
    assert ncl % 4 == 0
    slot_rows = t_lat // 2
    dh = M_HEAD_DIM
    kern = functools.partial(_mlstm_kernel, ncc=ncc, ncl=ncl)
    return pl.pallas_call(
        kern,
        grid=(batch, M_HEADS),
        in_specs=[
            pl.BlockSpec((None, t_lat, dh), lambda b, h: (h, b, 0)),
            pl.BlockSpec((dh, t_lat), lambda b, h: (h, b)),
            pl.BlockSpec((None, t_lat, dh), lambda b, h: (h, b, 0)),
            pl.BlockSpec((dh, t_ctx), lambda b, h: (h, b)),
            pl.BlockSpec((None, t_ctx, dh), lambda b, h: (h, b, 0)),
            pl.BlockSpec((4, M_HEADS, None, ncl, CHUNK), lambda b, h: (0, 0, b, 0, 0)),
            pl.BlockSpec((4, M_HEADS, None, ncc, CHUNK), lambda b, h: (0, 0, b, 0, 0)),
        ],
        out_specs=pl.BlockSpec((None, t_lat, dh), lambda b, h: (h, b, 0)),
        out_shape=jax.ShapeDtypeStruct((M_HEADS, batch * t_lat, dh), F32),
        scratch_shapes=[
            pltpu.VMEM((2, dh, dh), F32),
            pltpu.VMEM((2, dh, LANES), F32),
            pltpu.VMEM((M_HEADS, 2, nc, CHUNK), F32),
            pltpu.VMEM((M_HEADS, 2, nc, CHUNK), F32),
            pltpu.VMEM((M_HEADS, 2, nc, CHUNK), F32),
            pltpu.VMEM((M_HEADS, 2, nc, LANES), F32),
            pltpu.VMEM((M_HEADS, 2, nc, LANES), F32),
            pltpu.VMEM((2, slot_rows, dh), F32),
            pltpu.VMEM((2, slot_rows, LANES), F32),
            pltpu.VMEM((2, slot_rows, LANES), F32),
            pltpu.VMEM((2, slot_rows, LANES), F32),
        ],
        compiler_params=_params("parallel", "arbitrary"),
        name="mlstm",
    )(q, kt, v, ktc, vc, gl, gc)


ROW_BLOCK = 4
ROW_SUB = 8
GROUPS_PER_STEP = 2


def _fourier_kernel(x_ref, bd_ref, w2_ref, kc_ref, ks_ref, *refs, rows, scale, side):
    ns = len(side)
    o_ref, zr_ref, zi_ref = refs[ns], refs[2 * ns + 1], refs[2 * ns + 2]
    _run_side_casts(refs[:ns], refs[ns + 1:2 * ns + 1], side)
    gd = F_GROUP_DIM
    nb = ROW_BLOCK * GRID_W
    for blk in range(rows // ROW_BLOCK):
        pq = _dot(bd_ref[...], x_ref[blk * nb:(blk + 1) * nb, :])
        ar = pq[0:nb, :].astype(BF16)
        ai = pq[nb:, :].astype(BF16)
        rs = slice(blk * ROW_BLOCK, (blk + 1) * ROW_BLOCK)
        for g in range(GROUPS_PER_STEP):
            cs = slice(g * gd, (g + 1) * gd)
            z = _dot(jnp.concatenate([ar[:, cs], ai[:, cs]], axis=1), w2_ref[...])
            zr_ref[rs, :, cs] = z[:, 0:gd].reshape(ROW_BLOCK, GRID_W, gd)
            zi_ref[rs, :, cs] = z[:, gd:].reshape(ROW_BLOCK, GRID_W, gd)
    nw = GROUPS_PER_STEP * gd
    for j in range(GRID_W // ROW_SUB):
        js = slice(j * ROW_SUB, (j + 1) * ROW_SUB)
        zr = zr_ref[:, js, :].reshape(rows * ROW_SUB, nw).astype(BF16)
        zi = zi_ref[:, js, :].reshape(rows * ROW_SUB, nw).astype(BF16)
        y = _dot(kc_ref[...], zr) + _dot(ks_ref[...], zi)
        o_ref[:, js, :] = (y * scale).reshape(rows, ROW_SUB, nw)


def _dft_mats(n):
    k = np.arange(n)
    ang = 2.0 * np.pi * np.outer(k, k) / n
    return np.cos(ang), np.sin(ang)


def _fourier(xf, *, batch, t_lat, side_jobs):
    rows = t_lat // GRID_W
    assert rows % ROW_BLOCK == 0
    ck, sk = _dft_mats(F_GROUP_DIM)
    cc, sc = _dft_mats(GRID_W)
    cr, sr = _dft_mats(rows)
    eye_b = np.eye(ROW_BLOCK)
    eye_s = np.eye(ROW_SUB)
    tables = (
        np.concatenate([np.kron(eye_b, cc), np.kron(eye_b, -sc)], axis=0),
        np.block([[ck, -sk], [sk, ck]]),
        np.kron(cr, eye_s), np.kron(sr, eye_s),
    )
    bd, w2, kc, ks = (jnp.asarray(t, F32).astype(BF16) for t in tables)
    scale = 1.0 / math.sqrt(rows * GRID_W * F_GROUP_DIM)
    gsteps = F_GROUPS // GROUPS_PER_STEP
    jobs = side_jobs(batch * gsteps)
    s_in, s_out, s_shape = _side_specs(jobs, lambda b, g: b * gsteps + g)
    kern = functools.partial(_fourier_kernel, rows=rows, scale=scale, side=tuple(j.transpose for j in jobs))
    nw = GROUPS_PER_STEP * F_GROUP_DIM
    out, *side_out = pl.pallas_call(
        kern,
        grid=(batch, gsteps),
        in_specs=[
            pl.BlockSpec((t_lat, nw), lambda b, g: (b, g)),
            _const_spec(bd.shape), _const_spec(w2.shape), _const_spec(kc.shape), _const_spec(ks.shape),
            *s_in,
        ],
        out_specs=[pl.BlockSpec((None, rows, GRID_W, nw), lambda b, g: (b, 0, 0, g)), *s_out],
        out_shape=[jax.ShapeDtypeStruct((batch, rows, GRID_W, F_WIDTH), F32), *s_shape],
        scratch_shapes=[
            pltpu.VMEM((rows, GRID_W, nw), F32),
            pltpu.VMEM((rows, GRID_W, nw), F32),
        ],
        compiler_params=_params("arbitrary", "arbitrary"),
        name="fourier",
    )(xf, bd, w2, kc, ks, *[j.src for j in jobs])
    return (out.reshape(batch * t_lat, F_WIDTH), *side_out)


def _merge_ffn_kernel(x_ref, yf_ref, h_ref, o_ref, gf_ref, gm_ref, mod_ref, g_ref, hg_ref,
                      wf_ref, wm_ref, wo_ref, w13_ref, w2_ref, out_ref, hm_ref, hid_ref):
    dh = M_HEAD_DIM
    for hd in range(M_HEADS):
        sl = slice(hd * dh, (hd + 1) * dh)
        hn = _rms(h_ref[hd], hg_ref[:, sl])
        hm_ref[:, sl] = (jax.nn.sigmoid(o_ref[:, sl].astype(F32)) * hn).astype(BF16)
    y = (jax.nn.sigmoid(gf_ref[...].astype(F32)) * _dot(yf_ref[...].astype(BF16), wf_ref[...])
         + jax.nn.sigmoid(gm_ref[...].astype(F32)) * _dot(hm_ref[...], wm_ref[...]))
    out = _dot(y.astype(BF16), wo_ref[...])
    x1 = x_ref[...] + _rms(out, mod_ref[5:6, :] * g_ref[3:4, :])
    gain_pre = g_ref[4:5, :] * (1.0 + mod_ref[7:8, :])
    gain_post = 0.5 * mod_ref[8:9, :] * g_ref[5:6, :]
    out_ref[...] = _ffn_rows(x1, mod_ref[6:7, :], gain_pre, gain_post, w13_ref, w2_ref, hid_ref)


def _merge_ffn(x, yf, h, o, gf, gm, mod, g, hg, wf, wm, wo, w13, w2, *, tm, tiles_per_seq):
    n, d = x.shape
    tok = lambda width: pl.BlockSpec((tm, width), lambda i: (i, 0))
    return pl.pallas_call(
        _merge_ffn_kernel,
        grid=(n // tm,),
        in_specs=[
            tok(d), tok(F_WIDTH),
            pl.BlockSpec((M_HEADS, tm, M_HEAD_DIM), lambda i: (0, i, 0)),
            tok(M_WIDTH), tok(d), tok(d),
            pl.BlockSpec((None, N_ADA, d), lambda i: (i // tiles_per_seq, 0, 0)),
            *[_const_spec(a.shape) for a in (g, hg, wf, wm, wo, w13, w2)],
        ],
        out_specs=tok(d),
        out_shape=jax.ShapeDtypeStruct((n, d), F32),
        scratch_shapes=[pltpu.VMEM((tm, M_WIDTH), BF16), pltpu.VMEM((tm, FF_HALF), BF16)],
        compiler_params=_params("parallel"),
        name="merge_ffn_b",
    )(x, yf, h, o, gf, gm, mod, g, hg, wf, wm, wo, w13, w2)


def _tile(t, cap):
    tm = min(t, cap)
    assert t % tm == 0
    return tm


def kernel(x, c, ctx, c_ctx, w_ada, b_ada, norm_g, w13_a, w2_a, w_in, b_in, conv_w, conv_b,
           head_g, w_four, w_mproj, w_out, w13_b, w2_b):
    batch, t_lat, d = x.shape
    t_ctx = ctx.shape[1]
    assert d == D_MODEL and w_ada.shape[0] == 1, "single-layer kernel"
    assert batch + 1 <= MOD_ROWS and t_lat % (GRID_W * CHUNK // math.gcd(GRID_W, CHUNK)) == 0
    assert t_ctx % CHUNK == 0

    cvec = jnp.concatenate([c, c_ctx[None], jnp.zeros((MOD_ROWS - batch - 1, d), F32)], axis=0)
    first_jobs = lambda steps: (_row_cast(w13_a[0], steps), _row_cast(w2_a[0], steps))
    mod, w13a, w2a = _ada(cvec, w_ada[0], b_ada[0][None], first_jobs)
    g = norm_g[0]

    gpad = LANES - N_GATES
    bi = b_in[0][None]
    b_p = (bi[:, :COL_GATES], jnp.pad(bi[:, COL_GATES:COL_BR], ((0, 0), (0, gpad))), bi[:, COL_BR:])
    cw, cb, hg = conv_w[0], conv_b[0][None], head_g[0][None]
    w_in_jobs = functools.partial(_w_in_jobs, w_in[0].T)
    late_jobs = lambda steps: tuple(_row_cast(w[0], steps) for w in (w13_b, w2_b, w_four, w_mproj, w_out))

    tm_l = _tile(t_lat, 512)
    tm_c = _tile(t_ctx, 512)
    tps_l = t_lat // tm_l
    tps_c = t_ctx // tm_c
    xl = x.reshape(batch * t_lat, d)
    xc = ctx.reshape(batch * t_ctx, d)

    lat = dict(tm=tm_l, tiles_per_seq=tps_l, mod_row0=0, mod_step=1)
    con = dict(tm=tm_c, tiles_per_seq=tps_c, mod_row0=batch, mod_step=0)
    tm_f = _tile(t_lat, 2 * FFN_SUB)
    lat_ffn = dict(lat, tm=tm_f, tiles_per_seq=t_lat // tm_f)
    con_ffn = dict(con, tm=_tile(batch * t_ctx, 2 * FFN_SUB))

    hl, *w_p = _ffn(xl, mod, g, w13a, w2a, mod0=0, g0=0, name="ffn_a_lat", side_jobs=w_in_jobs, **lat_ffn)
    hc = _ffn(xc, mod, g, w13a, w2a, mod0=0, g0=0, name="ffn_a_ctx", **con_ffn)

    xf, q, kt, v, o, gt, gf, gm = _proj(hl, mod, g, w_p, b_p, cw, cb, full=True, name="proj_lat", **lat)
    ktc, vc, gtc = _proj(hc, mod, g, w_p, b_p, cw, cb, full=False, name="proj_ctx", **con)

    gl = gt.reshape(4, M_HEADS, batch, t_lat // CHUNK, CHUNK)
    gc = gtc.reshape(4, M_HEADS, batch, t_ctx // CHUNK, CHUNK)
    hm = _mlstm(q, kt, v, ktc, vc, gl, gc, batch=batch, t_lat=t_lat, t_ctx=t_ctx)
    yf, w13b, w2b, wf, wm, wo = _fourier(xf, batch=batch, t_lat=t_lat, side_jobs=late_jobs)

    hl = _merge_ffn(hl, yf, hm, o, gf, gm, mod, g, hg, wf, wm, wo, w13b, w2b, tm=tm_l, tiles_per_seq=tps_l)
    return hl.reshape(batch, t_lat, d)
```

```python
import functools
import math
from typing import Any, Callable, NamedTuple

import numpy as np
import jax
import jax.numpy as jnp
from jax import lax
from jax.experimental import pallas as pl
from jax.experimental.pallas import tpu as pltpu

D_MODEL = 1024
GRID_W = 64
FF_HALF = 2816
F_GROUPS = 4
F_GROUP_DIM = 128
F_WIDTH = F_GROUPS * F_GROUP_DIM
M_HEADS = 4
M_HEAD_DIM = 256
M_WIDTH = M_HEADS * M_HEAD_DIM
CHUNK = 128
N_ADA = 9
EPS = 1e-6
LOG2E = math.log2(math.e)

COL_Q = F_WIDTH
COL_K = COL_Q + M_WIDTH
COL_V = COL_K + M_WIDTH
COL_O = COL_V + M_WIDTH
COL_GATES = COL_O + M_WIDTH
N_GATES = 4 * M_HEADS
COL_BR = COL_GATES + N_GATES

LANES = 128
F32_SUBLANES = 8
BF16_SUBLANES = 16
MOD_ROWS = 8
HALO = BF16_SUBLANES
VMEM_LIMIT = 56 * 1024 * 1024

F32 = jnp.float32
BF16 = jnp.bfloat16


def _const_spec(shape):
    nd = len(shape)
    return pl.BlockSpec(shape, lambda *_: (0,) * nd, pipeline_mode=pl.Buffered(1))


def _params(*sem):
    return pltpu.CompilerParams(dimension_semantics=sem, vmem_limit_bytes=VMEM_LIMIT)


def _rms(x, g):
    return x * lax.rsqrt(jnp.mean(x * x, axis=-1, keepdims=True) + EPS) * g


def _dot(a, b):
    return jnp.dot(a, b, preferred_element_type=F32)


class _SideCast(NamedTuple):
    src: Any
    in_block: tuple
    in_index: Callable[[Any], tuple]
    out_shape: tuple
    out_block: tuple
    out_index: Callable[[Any], tuple]
    transpose: bool


def _num_blocks(units, steps):
    return max(k for k in range(1, min(units, steps) + 1) if units % k == 0)


def _row_cast(w, steps):
    rows, cols = w.shape
    nb = _num_blocks(rows // BF16_SUBLANES, steps)
    rb = rows // nb
    idx = lambda s: (jnp.minimum(s, nb - 1), 0)
    return _SideCast(w, (rb, cols), idx, (rows, cols), (rb, cols), idx, False)


def _transposed_cast(wt, row0, n, steps):
    nb = _num_blocks(n // LANES, steps)
    cb = n // nb
    d = wt.shape[1]
    assert row0 % F32_SUBLANES == 0
    row = lambda s: pl.multiple_of(row0 + jnp.minimum(s, nb - 1) * cb, F32_SUBLANES)
    return _SideCast(wt, (pl.Element(cb), pl.Element(d)), lambda s: (row(s), 0),
                     (d, n), (d, cb), lambda s: (0, jnp.minimum(s, nb - 1)), True)


def _side_specs(jobs, step_of):
    ins = [pl.BlockSpec(j.in_block, lambda *g, j=j: j.in_index(step_of(*g))) for j in jobs]
    outs = [pl.BlockSpec(j.out_block, lambda *g, j=j: j.out_index(step_of(*g))) for j in jobs]
    shapes = [jax.ShapeDtypeStruct(j.out_shape, BF16) for j in jobs]
    return ins, outs, shapes


def _run_side_casts(in_refs, out_refs, transposes):
    for i_ref, o_ref, transpose in zip(in_refs, out_refs, transposes):
        v = i_ref[...]
        o_ref[...] = (v.T if transpose else v).astype(BF16)


def _ada_kernel(c_ref, w_ref, b_ref, *refs, side):
    ns = len(side)
    _run_side_casts(refs[:ns], refs[ns + 1:], side)
    c = c_ref[...]
    s = (c * jax.nn.sigmoid(c)).astype(BF16)
    refs[ns][...] = _dot(s, w_ref[...].astype(BF16)) + b_ref[...]


def _ada(cvec, w, b, side_jobs):
    d = cvec.shape[1]
    jobs = side_jobs(N_ADA)
    s_in, s_out, s_shape = _side_specs(jobs, lambda j: j)
    out, *side_out = pl.pallas_call(
        functools.partial(_ada_kernel, side=tuple(j.transpose for j in jobs)),
        grid=(N_ADA,),
        in_specs=[
            pl.BlockSpec((MOD_ROWS, d), lambda j: (0, 0)),
            pl.BlockSpec((d, d), lambda j: (0, j)),
            pl.BlockSpec((1, d), lambda j: (0, j)),
            *s_in,
        ],
        out_specs=[pl.BlockSpec((MOD_ROWS, d), lambda j: (0, j)), *s_out],
        out_shape=[jax.ShapeDtypeStruct((MOD_ROWS, N_ADA * d), F32), *s_shape],
        compiler_params=_params("arbitrary"),
        name="ada",
    )(cvec, w, b, *[j.src for j in jobs])
    return (out.reshape(MOD_ROWS, N_ADA, d), *side_out)


FF_CHUNK = 256
FFN_SUB = 512


def _ffn_kernel(x_ref, mod_ref, g_ref, w13_ref, w2_ref, *refs, mod0, g0, side):
    ns = len(side)
    o_ref, hid_ref = refs[ns], refs[2 * ns + 1]
    _run_side_casts(refs[:ns], refs[ns + 1:2 * ns + 1], side)
    shift = mod_ref[mod0:mod0 + 1, :]
    gain_pre = g_ref[g0:g0 + 1, :] * (1.0 + mod_ref[mod0 + 1:mod0 + 2, :])
    gain_post = 0.5 * mod_ref[mod0 + 2:mod0 + 3, :] * g_ref[g0 + 1:g0 + 2, :]
    nsub, sub, _ = hid_ref.shape
    rows = lambda s: slice(s * sub, (s + 1) * sub)

    for s in range(nsub):
        o_ref[rows(s), :] = _ffn_rows(x_ref[rows(s), :], shift, gain_pre, gain_post, w13_ref, w2_ref, hid_ref.at[s])


def _ffn_rows(x, shift, gain_pre, gain_post, w13_ref, w2_ref, hid_ref):
    u = (_rms(x, gain_pre) + shift).astype(BF16)
    for c in range(FF_HALF // FF_CHUNK):
        lo = c * FF_CHUNK
        a = _dot(u, w13_ref[:, lo:lo + FF_CHUNK])
        b = _dot(u, w13_ref[:, FF_HALF + lo:FF_HALF + lo + FF_CHUNK])
        hid_ref[:, lo:lo + FF_CHUNK] = (a * jax.nn.sigmoid(a) * b).astype(BF16)
    return x + _rms(_dot(hid_ref[...], w2_ref[...]), gain_post)


def _ffn(h, mod, g, w13, w2, *, tm, tiles_per_seq, mod_row0, mod_step, mod0, g0, name, side_jobs=None):
    n, d = h.shape
    jobs = side_jobs(n // tm) if side_jobs else ()
    s_in, s_out, s_shape = _side_specs(jobs, lambda i: i)
    kern = functools.partial(_ffn_kernel, mod0=mod0, g0=g0, side=tuple(j.transpose for j in jobs))
    sub = min(tm, FFN_SUB)
    outs = pl.pallas_call(
        kern,
        grid=(n // tm,),
        in_specs=[
            pl.BlockSpec((tm, d), lambda i: (i, 0)),
            pl.BlockSpec((None, N_ADA, d), lambda i: (mod_row0 + mod_step * (i // tiles_per_seq), 0, 0)),
            _const_spec(g.shape),
            _const_spec(w13.shape),
            _const_spec(w2.shape),
            *s_in,
        ],
        out_specs=[pl.BlockSpec((tm, d), lambda i: (i, 0)), *s_out],
        out_shape=[jax.ShapeDtypeStruct((n, d), F32), *s_shape],
        scratch_shapes=[pltpu.VMEM((tm // sub, sub, FF_HALF), BF16)],
        compiler_params=_params("arbitrary"),
        name=name,
    )(h, mod, g, w13, w2, *[j.src for j in jobs])
    return outs if jobs else outs[0]


QK_CHUNK = 256


def _w_in_jobs(wt, steps):
    return (_transposed_cast(wt, 0, COL_GATES, steps),
            _transposed_cast(wt, COL_GATES, LANES, steps),
            _transposed_cast(wt, COL_BR, wt.shape[0] - COL_BR, steps))


def _proj_kernel(x_ref, xp_ref, xn_ref, mod_ref, g_ref, w_ref, wg_ref, wbr_ref, b_ref,
                 cw_ref, cb_ref, *refs, tm, tiles_per_seq, full):
    if full:
        xf_ref, q_ref, kt_ref, v_ref, o_ref, gt_ref, gf_ref, gm_ref, u_ref, p_ref = refs
    else:
        kt_ref, v_ref, gt_ref, u_ref, p_ref = refs
    i = pl.program_id(0)
    first = (i % tiles_per_seq) == 0
    last = (i % tiles_per_seq) == tiles_per_seq - 1
    shift = mod_ref[3:4, :]
    gain = g_ref[2:3, :] * (1.0 + mod_ref[4:5, :])

    def mk_u(x):
        return (_rms(x, gain) + shift).astype(BF16)

    u_ref[0:HALO, :] = mk_u(xp_ref[...])
    u_ref[HALO:HALO + tm, :] = mk_u(x_ref[...])
    u_ref[HALO + tm:, :] = mk_u(xn_ref[...])
    u = u_ref[HALO:HALO + tm, :]

    def lin(dst_ref, src_ref, bias_lo, lo, k):
        cols = slice(lo + k, lo + k + QK_CHUNK)
        bias = b_ref[:, bias_lo + k:bias_lo + k + QK_CHUNK]
        val = (_dot(u, src_ref[:, cols]) + bias).astype(dst_ref.dtype)
        if len(dst_ref.shape) == 3:
            dst_ref[k // QK_CHUNK] = val
        else:
            dst_ref[:, k:k + QK_CHUNK] = val

    def gates():
        gt_ref[...] = (_dot(u, wg_ref[...]) + b_ref[:, COL_GATES:COL_GATES + LANES]).T[0:N_GATES, :]

    top = slice(HALO - F32_SUBLANES, HALO)
    bot = slice(HALO + tm, HALO + tm + F32_SUBLANES)

    def conv(c):
        lo = c * QK_CHUNK
        pb = p_ref.at[c % 2]
        p = _dot(u_ref[...], w_ref[:, COL_Q + lo:COL_Q + lo + QK_CHUNK]) + b_ref[:, COL_Q + lo:COL_Q + lo + QK_CHUNK]
        pb[...] = p
        pb[top, :] = jnp.where(first, 0.0, p[top, :])
        pb[bot, :] = jnp.where(last, 0.0, p[bot, :])
        cw = cw_ref[:, lo:lo + QK_CHUNK]
        y = (cw[0:1, :] * pb[HALO - 1:HALO - 1 + tm, :] + cw[1:2, :] * pb[HALO:HALO + tm, :]
             + cw[2:3, :] * pb[HALO + 1:HALO + 1 + tm, :]) + cb_ref[:, lo:lo + QK_CHUNK]
        y = y * jax.nn.sigmoid(y)
        if lo < M_WIDTH:
            q_ref[lo // QK_CHUNK] = y.astype(BF16)
        else:
            kt_ref[lo - M_WIDTH:lo - M_WIDTH + QK_CHUNK, :] = (y * (M_HEAD_DIM ** -0.5)).T.astype(BF16)

    chunks = lambda width: range(0, width, QK_CHUNK)
    plain = [functools.partial(lin, v_ref, w_ref, COL_V, COL_V, k) for k in chunks(M_WIDTH)] + [gates]
    if full:
        plain += [functools.partial(lin, xf_ref, w_ref, 0, 0, k) for k in chunks(F_WIDTH)]
        plain += [functools.partial(lin, o_ref, w_ref, COL_O, COL_O, k) for k in chunks(M_WIDTH)]
        plain += [functools.partial(lin, gf_ref, wbr_ref, COL_BR, 0, k) for k in chunks(D_MODEL)]
        plain += [functools.partial(lin, gm_ref, wbr_ref, COL_BR + D_MODEL, D_MODEL, k) for k in chunks(D_MODEL)]
    convs = list(range(0 if full else M_WIDTH // QK_CHUNK, 2 * M_WIDTH // QK_CHUNK))
    per_conv = -(-len(plain) // len(convs))
    for n_done, c in enumerate(convs):
        conv(c)
        for task in plain[n_done * per_conv:(n_done + 1) * per_conv]:
            task()


def _proj(h, mod, g, ws, b, cw, cb, *, tm, tiles_per_seq, mod_row0, mod_step, full, name):
    n, d = h.shape
    hb = tm // HALO
    nhb = n // HALO
    kern = functools.partial(_proj_kernel, tm=tm, tiles_per_seq=tiles_per_seq, full=full)
    tok = lambda width, dt: (pl.BlockSpec((tm, width), lambda i: (i, 0)), jax.ShapeDtypeStruct((n, width), dt))
    tr = lambda width, dt: (pl.BlockSpec((width, tm), lambda i: (0, i)), jax.ShapeDtypeStruct((width, n), dt))
    assert QK_CHUNK == M_HEAD_DIM
    heads = (pl.BlockSpec((M_HEADS, tm, M_HEAD_DIM), lambda i: (0, i, 0)),
             jax.ShapeDtypeStruct((M_HEADS, n, M_HEAD_DIM), BF16))
    xf, o, gf, gm = tok(F_WIDTH, BF16), tok(M_WIDTH, BF16), tok(d, BF16), tok(d, BF16)
    q = v = heads
    kt = tr(M_WIDTH, BF16)
    gt = tr(N_GATES, F32)
    outs = (xf, q, kt, v, o, gt, gf, gm) if full else (kt, v, gt)
    return pl.pallas_call(
        kern,
        grid=(n // tm,),
        in_specs=[
            pl.BlockSpec((tm, d), lambda i: (i, 0)),
            pl.BlockSpec((HALO, d), lambda i: (jnp.maximum(i * hb - 1, 0), 0)),
            pl.BlockSpec((HALO, d), lambda i: (jnp.minimum((i + 1) * hb, nhb - 1), 0)),
            pl.BlockSpec((None, N_ADA, d), lambda i: (mod_row0 + mod_step * (i // tiles_per_seq), 0, 0)),
            _const_spec(g.shape),
            *[_const_spec(a.shape) for a in (*ws, b, cw, cb)],
        ],
        out_specs=[s for s, _ in outs],
        out_shape=[s for _, s in outs],
        scratch_shapes=[pltpu.VMEM((tm + 2 * HALO, d), BF16),
                        pltpu.VMEM((2, tm + 2 * HALO, QK_CHUNK), F32)],
        compiler_params=_params("parallel"),
        name=name,
    )(h, h, h, mod, g, *ws, b, cw, cb)


def _lane_scan(x, reverse):
    lane = lax.broadcasted_iota(jnp.int32, x.shape, 1)
    k = 1
    while k < CHUNK:
        if reverse:
            x = x + jnp.where(lane < CHUNK - k, pltpu.roll(x, CHUNK - k, 1), 0.0)
        else:
            x = x + jnp.where(lane >= k, pltpu.roll(x, k, 1), 0.0)
        k *= 2
    return x


def _mlstm_kernel(q_ref, kt_ref, v_ref, ktc_ref, vc_ref, gl_ref, gc_ref, o_ref,
                  c_ref, n_ref, r_all, wk_all, lf_all, ms_all, dec_all,
                  num_ref, den_ref, a_ref, e_ref, *, ncc, ncl):
    L = CHUNK
    dv = M_HEAD_DIM
    half = ncl // 2
    t_idx = lax.broadcasted_iota(jnp.int32, (L, L), 0)
    s_idx = lax.broadcasted_iota(jnp.int32, (L, L), 1)
    masks = (s_idx <= t_idx, s_idx >= t_idx)

    def gate_rows(g_ref, hd, d):
        li = g_ref[2 * d, hd] * LOG2E
        gf = g_ref[2 * d + 1, hd]
        lf = (jnp.minimum(gf, 0.0) - jnp.log1p(jnp.exp(-jnp.abs(gf)))) * LOG2E
        bcum = _lane_scan(lf, reverse=(d == 1))
        r = li - bcum
        rmax = jnp.max(r, axis=-1, keepdims=True)
        b_end = bcum[:, L - 1:L] if d == 0 else bcum[:, 0:1]
        return lf, r, rmax, b_end

    def gate_prologue(hd, d):
        parts = (gate_rows(gl_ref, hd, d), gate_rows(gc_ref, hd, d))
        rev = d == 1
        order = [(1, c) for c in (range(ncc - 1, -1, -1) if rev else range(ncc))]
        order += [(0, c) for c in (range(ncl - 1, -1, -1) if rev else range(ncl))]
        m = jnp.zeros((1, 1), F32)
        for part, c in order:
            row = c + (ncl if part == 1 else 0)
            ms_all[hd, d, row:row + 1, :] = jnp.broadcast_to(m, (1, LANES))
            _, _, rmax, b_end = parts[part]
            m = b_end[c:c + 1, :] + jnp.maximum(m, rmax[c:c + 1, :])
        for part, (lo, cnt) in enumerate(((0, ncl), (ncl, ncc))):
            lf, r, rmax, _ = parts[part]
            m_start = ms_all[hd, d, lo:lo + cnt, 0:1]
            m_end = jnp.maximum(m_start, rmax)
            dec_all[hd, d, lo:lo + cnt, :] = jnp.broadcast_to(jnp.exp2(m_start - m_end), (cnt, LANES))
            wk_all[hd, d, lo:lo + cnt, :] = jnp.exp2(r - m_end)
            r_all[hd, d, lo:lo + cnt, :] = r
            lf_all[hd, d, lo:lo + cnt, :] = lf

    head = pl.program_id(1)

    @pl.when(head == 0)
    def _():
        for hd in range(M_HEADS):
            for d in range(2):
                gate_prologue(hd, d)

    r_ref, wk_ref, lf_ref, ms_ref, dec_ref = (a.at[head] for a in (r_all, wk_all, lf_all, ms_all, dec_all))

    c_ref[...] = jnp.zeros_like(c_ref)
    n_ref[...] = jnp.zeros_like(n_ref)

    def update_state(d, row, kt_c, v_c):
        wk = wk_ref[d, pl.ds(row, 1), :]
        dec = dec_ref[d, pl.ds(row, 1), :]
        ktw = kt_c.astype(F32) * wk
        c_ref[d] = jnp.concatenate([dec] * (dv // LANES), axis=1) * c_ref[d] + _dot(ktw.astype(BF16), v_c)
        n_ref[d] = dec * n_ref[d] + jnp.sum(ktw, axis=-1, keepdims=True)

    def chunk_refs(c):
        off = pl.multiple_of(c * L, L)
        return off, q_ref[pl.ds(off, L), :], kt_ref[:, pl.ds(off, L)], v_ref[pl.ds(off, L), :]

    def intra(d, c, slot):
        _, q_c, kt_c, v_c = chunk_refs(c)
        rows = pl.ds(pl.multiple_of(slot * L, L), L)
        r_row = r_ref[d, pl.ds(c, 1), :]
        lf_row = lf_ref[d, pl.ds(c, 1), :]
        m = ms_ref[d, pl.ds(c, 1), :]
        rm = jnp.where(masks[d], r_row, -jnp.inf)
        mt = jnp.maximum(m, jnp.max(rm, axis=-1, keepdims=True))
        bc = jnp.sum(jnp.where(masks[d], lf_row, 0.0), axis=-1, keepdims=True)
        s = _dot(q_c, kt_c) * jnp.exp2(rm - mt)
        num_ref[d, rows, :] = _dot(s.astype(BF16), v_c)
        den_ref[d, rows, :] = jnp.broadcast_to(jnp.sum(s, axis=-1, keepdims=True), (L, LANES))
        a_ref[d, rows, :] = jnp.exp2(m - mt)
        e_ref[d, rows, :] = jnp.exp2(-bc - mt)

    def inter(d, c, slot, accumulate):
        off, q_c, kt_c, v_c = chunk_refs(c)
        rows = pl.ds(pl.multiple_of(slot * L, L), L)
        a = a_ref[d, rows, :]
        state = jnp.concatenate([c_ref[d].astype(BF16), n_ref[d].astype(BF16)], axis=1)
        qcn = _dot(q_c, state)
        den = a * qcn[:, dv:] + den_ref[d, rows, :]
        inv = 1.0 / jnp.maximum(jnp.abs(den), e_ref[d, rows, :])
        for k in range(0, dv, LANES):
            h = (a * qcn[:, k:k + LANES] + num_ref[d, rows, k:k + LANES]) * inv
            if accumulate:
                o_ref[pl.ds(off, L), k:k + LANES] += h
            else:
                o_ref[pl.ds(off, L), k:k + LANES] = h
        update_state(d, c, kt_c, v_c)

    for d in range(2):
        for c in (range(ncc) if d == 0 else range(ncc - 1, -1, -1)):
            update_state(d, ncl + c, ktc_ref[:, c * L:(c + 1) * L], vc_ref[c * L:(c + 1) * L, :])

    def chunk_of(d, g):
        return g if d == 0 else ncl - 1 - g

    def intra_step(g, carry):
        for d in range(2):
            intra(d, chunk_of(d, g), g)
        return carry

    def inter_step(accumulate):
        def body(g, carry):
            for d in range(2):
                inter(d, chunk_of(d, g), g, accumulate)
            return carry
        return body

    lax.fori_loop(0, ncl, intra_step, 0, unroll=8)
    lax.fori_loop(0, half, inter_step(False), 0, unroll=4)
    lax.fori_loop(half, ncl, inter_step(True), 0, unroll=4)


def _mlstm(q, kt, v, ktc, vc, gl, gc, *, batch, t_lat, t_ctx):
    ncl = t_lat // CHUNK
    ncc = t_ctx // CHUNK
    nc = ncc + ncl
    assert ncl % 4 == 0
    slot_rows = t_lat
    dh = M_HEAD_DIM
    kern = functools.partial(_mlstm_kernel, ncc=ncc, ncl=ncl)
    return pl.pallas_call(
        kern,
        grid=(batch, M_HEADS),
        in_specs=[
            pl.BlockSpec((None, t_lat, dh), lambda b, h: (h, b, 0)),
            pl.BlockSpec((dh, t_lat), lambda b, h: (h, b)),
            pl.BlockSpec((None, t_lat, dh), lambda b, h: (h, b, 0)),
            pl.BlockSpec((dh, t_ctx), lambda b, h: (h, b)),
            pl.BlockSpec((None, t_ctx, dh), lambda b, h: (h, b, 0)),
            pl.BlockSpec((4, M_HEADS, None, ncl, CHUNK), lambda b, h: (0, 0, b, 0, 0)),
            pl.BlockSpec((4, M_HEADS, None, ncc, CHUNK), lambda b, h: (0, 0, b, 0, 0)),
        ],
        out_specs=pl.BlockSpec((None, t_lat, dh), lambda b, h: (h, b, 0)),
        out_shape=jax.ShapeDtypeStruct((M_HEADS, batch * t_lat, dh), F32),
        scratch_shapes=[
            pltpu.VMEM((2, dh, dh), F32),
            pltpu.VMEM((2, dh, LANES), F32),
            pltpu.VMEM((M_HEADS, 2, nc, CHUNK), F32),
            pltpu.VMEM((M_HEADS, 2, nc, CHUNK), F32),
            pltpu.VMEM((M_HEADS, 2, nc, CHUNK), F32),
            pltpu.VMEM((M_HEADS, 2, nc, LANES), F32),
            pltpu.VMEM((M_HEADS, 2, nc, LANES), F32),
            pltpu.VMEM((2, slot_rows, dh), F32),
            pltpu.VMEM((2, slot_rows, LANES), F32),
            pltpu.VMEM((2, slot_rows, LANES), F32),
            pltpu.VMEM((2, slot_rows, LANES), F32),
        ],
        compiler_params=_params("parallel", "arbitrary"),
        name="mlstm",
    )(q, kt, v, ktc, vc, gl, gc)


ROW_BLOCK = 4
ROW_SUB = 8
GROUPS_PER_STEP = 2


def _fourier_kernel(x_ref, bd_ref, w2_ref, kc_ref, ks_ref, *refs, rows, scale, side):
    ns = len(side)
    o_ref, zr_ref, zi_ref = refs[ns], refs[2 * ns + 1], refs[2 * ns + 2]
    _run_side_casts(refs[:ns], refs[ns + 1:2 * ns + 1], side)
    gd = F_GROUP_DIM
    nb = ROW_BLOCK * GRID_W
    for blk in range(rows // ROW_BLOCK):
        pq = _dot(bd_ref[...], x_ref[blk * nb:(blk + 1) * nb, :])
        ar = pq[0:nb, :].astype(BF16)
        ai = pq[nb:, :].astype(BF16)
        rs = slice(blk * ROW_BLOCK, (blk + 1) * ROW_BLOCK)
        for g in range(GROUPS_PER_STEP):
            cs = slice(g * gd, (g + 1) * gd)
            z = _dot(jnp.concatenate([ar[:, cs], ai[:, cs]], axis=1), w2_ref[...])
            zr_ref[rs, :, cs] = z[:, 0:gd].reshape(ROW_BLOCK, GRID_W, gd)
            zi_ref[rs, :, cs] = z[:, gd:].reshape(ROW_BLOCK, GRID_W, gd)
    nw = GROUPS_PER_STEP * gd
    for j in range(GRID_W // ROW_SUB):
        js = slice(j * ROW_SUB, (j + 1) * ROW_SUB)
        zr = zr_ref[:, js, :].reshape(rows * ROW_SUB, nw).astype(BF16)
        zi = zi_ref[:, js, :].reshape(rows * ROW_SUB, nw).astype(BF16)
        y = _dot(kc_ref[...], zr) + _dot(ks_ref[...], zi)
        o_ref[:, js, :] = (y * scale).reshape(rows, ROW_SUB, nw)


def _dft_mats(n):
    k = np.arange(n)
    ang = 2.0 * np.pi * np.outer(k, k) / n
    return np.cos(ang), np.sin(ang)


def _fourier(xf, *, batch, t_lat, side_jobs):
    rows = t_lat // GRID_W
    assert rows % ROW_BLOCK == 0
    ck, sk = _dft_mats(F_GROUP_DIM)
    cc, sc = _dft_mats(GRID_W)
    cr, sr = _dft_mats(rows)
    eye_b = np.eye(ROW_BLOCK)
    eye_s = np.eye(ROW_SUB)
    tables = (
        np.concatenate([np.kron(eye_b, cc), np.kron(eye_b, -sc)], axis=0),
        np.block([[ck, -sk], [sk, ck]]),
        np.kron(cr, eye_s), np.kron(sr, eye_s),
    )
    bd, w2, kc, ks = (jnp.asarray(t, F32).astype(BF16) for t in tables)
    scale = 1.0 / math.sqrt(rows * GRID_W * F_GROUP_DIM)
    gsteps = F_GROUPS // GROUPS_PER_STEP
    jobs = side_jobs(batch * gsteps)
    s_in, s_out, s_shape = _side_specs(jobs, lambda b, g: b * gsteps + g)
    kern = functools.partial(_fourier_kernel, rows=rows, scale=scale, side=tuple(j.transpose for j in jobs))
    nw = GROUPS_PER_STEP * F_GROUP_DIM
    out, *side_out = pl.pallas_call(
        kern,
        grid=(batch, gsteps),
        in_specs=[
            pl.BlockSpec((t_lat, nw), lambda b, g: (b, g)),
            _const_spec(bd.shape), _const_spec(w2.shape), _const_spec(kc.shape), _const_spec(ks.shape),
            *s_in,
        ],
        out_specs=[pl.BlockSpec((None, rows, GRID_W, nw), lambda b, g: (b, 0, 0, g)), *s_out],
        out_shape=[jax.ShapeDtypeStruct((batch, rows, GRID_W, F_WIDTH), F32), *s_shape],
        scratch_shapes=[
            pltpu.VMEM((rows, GRID_W, nw), F32),
            pltpu.VMEM((rows, GRID_W, nw), F32),
        ],
        compiler_params=_params("arbitrary", "arbitrary"),
        name="fourier",
    )(xf, bd, w2, kc, ks, *[j.src for j in jobs])
    return (out.reshape(batch * t_lat, F_WIDTH), *side_out)


def _merge_ffn_kernel(x_ref, yf_ref, h_ref, o_ref, gf_ref, gm_ref, mod_ref, g_ref, hg_ref,
                      wf_ref, wm_ref, wo_ref, w13_ref, w2_ref, out_ref, hm_ref, hid_ref):
    dh = M_HEAD_DIM
    for hd in range(M_HEADS):
        sl = slice(hd * dh, (hd + 1) * dh)
        hn = _rms(h_ref[hd], hg_ref[:, sl])
        hm_ref[:, sl] = (jax.nn.sigmoid(o_ref[:, sl].astype(F32)) * hn).astype(BF16)
    y = (jax.nn.sigmoid(gf_ref[...].astype(F32)) * _dot(yf_ref[...].astype(BF16), wf_ref[...])
         + jax.nn.sigmoid(gm_ref[...].astype(F32)) * _dot(hm_ref[...], wm_ref[...]))
    out = _dot(y.astype(BF16), wo_ref[...])
    x1 = x_ref[...] + _rms(out, mod_ref[5:6, :] * g_ref[3:4, :])
    gain_pre = g_ref[4:5, :] * (1.0 + mod_ref[7:8, :])
    gain_post = 0.5 * mod_ref[8:9, :] * g_ref[5:6, :]
    out_ref[...] = _ffn_rows(x1, mod_ref[6:7, :], gain_pre, gain_post, w13_ref, w2_ref, hid_ref)


def _merge_ffn(x, yf, h, o, gf, gm, mod, g, hg, wf, wm, wo, w13, w2, *, tm, tiles_per_seq):
    n, d = x.shape
    tok = lambda width: pl.BlockSpec((tm, width), lambda i: (i, 0))
    return pl.pallas_call(
        _merge_ffn_kernel,
        grid=(n // tm,),
        in_specs=[
            tok(d), tok(F_WIDTH),
            pl.BlockSpec((M_HEADS, tm, M_HEAD_DIM), lambda i: (0, i, 0)),
            tok(M_WIDTH), tok(d), tok(d),
            pl.BlockSpec((None, N_ADA, d), lambda i: (i // tiles_per_seq, 0, 0)),
            *[_const_spec(a.shape) for a in (g, hg, wf, wm, wo, w13, w2)],
        ],
        out_specs=tok(d),
        out_shape=jax.ShapeDtypeStruct((n, d), F32),
        scratch_shapes=[pltpu.VMEM((tm, M_WIDTH), BF16), pltpu.VMEM((tm, FF_HALF), BF16)],
        compiler_params=_params("parallel"),
        name="merge_ffn_b",
    )(x, yf, h, o, gf, gm, mod, g, hg, wf, wm, wo, w13, w2)


def _tile(t, cap):
    tm = min(t, cap)
    assert t % tm == 0
    return tm


def kernel(x, c, ctx, c_ctx, w_ada, b_ada, norm_g, w13_a, w2_a, w_in, b_in, conv_w, conv_b,
           head_g, w_four, w_mproj, w_out, w13_b, w2_b):
    batch, t_lat, d = x.shape
    t_ctx = ctx.shape[1]
    assert d == D_MODEL and w_ada.shape[0] == 1, "single-layer kernel"
    assert batch + 1 <= MOD_ROWS and t_lat % (GRID_W * CHUNK // math.gcd(GRID_W, CHUNK)) == 0
    assert t_ctx % CHUNK == 0

    cvec = jnp.concatenate([c, c_ctx[None], jnp.zeros((MOD_ROWS - batch - 1, d), F32)], axis=0)
    first_jobs = lambda steps: (_row_cast(w13_a[0], steps), _row_cast(w2_a[0], steps))
    mod, w13a, w2a = _ada(cvec, w_ada[0], b_ada[0][None], first_jobs)
    g = norm_g[0]

    b_p = b_in[0][None]
    cw, cb, hg = conv_w[0], conv_b[0][None], head_g[0][None]
    w_in_jobs = functools.partial(_w_in_jobs, w_in[0].T)
    late_jobs = lambda steps: tuple(_row_cast(w[0], steps) for w in (w13_b, w2_b, w_four, w_mproj, w_out))

    tm_l = _tile(t_lat, 512)
    tm_c = _tile(t_ctx, 512)
    tps_l = t_lat // tm_l
    tps_c = t_ctx // tm_c
    xl = x.reshape(batch * t_lat, d)
    xc = ctx.reshape(batch * t_ctx, d)

    lat = dict(tm=tm_l, tiles_per_seq=tps_l, mod_row0=0, mod_step=1)
    con = dict(tm=tm_c, tiles_per_seq=tps_c, mod_row0=batch, mod_step=0)
    tm_f = _tile(t_lat, 2 * FFN_SUB)
    lat_ffn = dict(lat, tm=tm_f, tiles_per_seq=t_lat // tm_f)
    con_ffn = dict(con, tm=_tile(batch * t_ctx, 2 * FFN_SUB))

    hl, *w_p = _ffn(xl, mod, g, w13a, w2a, mod0=0, g0=0, name="ffn_a_lat", side_jobs=w_in_jobs, **lat_ffn)
    hc = _ffn(xc, mod, g, w13a, w2a, mod0=0, g0=0, name="ffn_a_ctx", **con_ffn)

    xf, q, kt, v, o, gt, gf, gm = _proj(hl, mod, g, w_p, b_p, cw, cb, full=True, name="proj_lat", **lat)
    ktc, vc, gtc = _proj(hc, mod, g, w_p, b_p, cw, cb, full=False, name="proj_ctx", **con)

    gl = gt.reshape(4, M_HEADS, batch, t_lat // CHUNK, CHUNK)
    gc = gtc.reshape(4, M_HEADS, batch, t_ctx // CHUNK, CHUNK)
    hm = _mlstm(q, kt, v, ktc, vc, gl, gc, batch=batch, t_lat=t_lat, t_ctx=t_ctx)
    yf, w13b, w2b, wf, wm, wo = _fourier(xf, batch=batch, t_lat=t_lat, side_jobs=late_jobs)

    hl = _merge_ffn(hl, yf, hm, o, gf, gm, mod, g, hg, wf, wm, wo, w13b, w2b, tm=tm_l, tiles_per_seq=tps_l)
    return hl.reshape(batch, t_lat, d)
```
